```python
import math
import jax
import jax.numpy as jnp
from jax import lax
import numpy as np

D_MODEL = 4096
BATCH = 1
SEQ = 8192
DEPTH = 4

GRID_W = 64
CTX_LEN = 256
CHUNK = 128
NORM_EPS = 1e-6
NEG_INF = -1e30
MOD_RANK = D_MODEL // 16
SSD_D_INNER = D_MODEL // 2
SSD_HEAD_DIM = 64
SSD_HEADS = SSD_D_INNER // SSD_HEAD_DIM
SSD_GROUPS = 4
SSD_HPG = SSD_HEADS // SSD_GROUPS
SSD_STATE = 128
SSD_CONV = 5
RET_HEADS = 8
RET_QK_DIM = 128
RET_V_DIM = 2 * RET_QK_DIM
ATTN_HEADS = 16
ATTN_KV_HEADS = 4
ATTN_HEAD_DIM = 128
WINDOW = 128
ROPE_DIM = 128
ROPE_BASE = 10000.0
N_BRANCH = 3
GATE_RANK = D_MODEL // 16
N_GROUPS = 8
EXPERTS_PER_GROUP = 8
N_EXPERTS = N_GROUPS * EXPERTS_PER_GROUP
EXPERT_FF = D_MODEL // 32
TOP_K = 2
SSD_XBC = SSD_D_INNER + 2 * SSD_GROUPS * SSD_STATE
RET_Q = RET_HEADS * RET_QK_DIM
RET_V = RET_HEADS * RET_V_DIM
ATTN_Q = ATTN_HEADS * ATTN_HEAD_DIM
ATTN_KV = ATTN_KV_HEADS * ATTN_HEAD_DIM
IN_SPLITS = (SSD_D_INNER, SSD_XBC, 2 * SSD_HEADS, RET_Q, RET_Q, RET_V, RET_V, ATTN_Q, ATTN_KV, ATTN_KV, GATE_RANK)
IN_WIDTH = sum(IN_SPLITS)

kernel_name = 'hybrid_ssd_retention_swa_hmoe_trunk'

F32 = jnp.float32


def split_cols(x, widths):
    idx, acc = [], 0
    for w in widths[:-1]:
        acc += w
        idx.append(acc)
    return jnp.split(x, idx, axis=-1)


def rmsnorm(x, g):
    xf = x.astype(F32)
    y = xf * lax.rsqrt(jnp.mean(xf * xf, axis=-1, keepdims=True) + NORM_EPS)
    return (y * g.astype(F32)).astype(x.dtype)


def modulate(x, g, shift, scale):
    y = rmsnorm(x, g).astype(F32) * (1.0 + scale.astype(F32)) + shift.astype(F32)
    return y.astype(x.dtype)


def axial_rope_tables(seq_len):
    n_rows = seq_len // GRID_W
    row = jnp.repeat(jnp.arange(n_rows), GRID_W).astype(F32)
    col = jnp.tile(jnp.arange(GRID_W), n_rows).astype(F32)
    n_freq = ROPE_DIM // 4
    inv = ROPE_BASE ** (-jnp.arange(n_freq, dtype=F32) / n_freq)

    def axis_angles(pos):
        ang = pos[:, None] * inv[None, :]
        return jnp.concatenate([ang, ang], axis=-1)

    ang = jnp.concatenate([axis_angles(row), axis_angles(col)], axis=-1)
    return jnp.cos(ang)[:, None, :], jnp.sin(ang)[:, None, :]


def apply_rope(x, cos, sin):
    h, q = ROPE_DIM // 2, ROPE_DIM // 4
    xr, xc = x[..., :h], x[..., h:]
    rot = jnp.concatenate([-xr[..., q:], xr[..., :q], -xc[..., q:], xc[..., :q]], axis=-1)
    return (x.astype(F32) * cos + rot.astype(F32) * sin).astype(x.dtype)


def dwconv_silu(x, w, bias):
    ch, k = x.shape[-1], w.shape[0]
    y = lax.conv_general_dilated(x, w.astype(x.dtype)[:, None, :], (1,), [(k // 2, k // 2)],
                                 dimension_numbers=('NWC', 'WIO', 'NWC'), feature_group_count=ch)
    return jax.nn.silu(y + bias.astype(x.dtype))


def chunked_scan(q, k, v, log_a, state0):
    b, l, g, n = q.shape
    r, p = v.shape[-2:]
    nc = l // CHUNK
    qc = q.reshape(b, nc, CHUNK, g, n)
    kc = k.reshape(b, nc, CHUNK, g, n)
    vc = v.reshape(b, nc, CHUNK, g, r, p)
    acs = jnp.cumsum(log_a.astype(F32).reshape(b, nc, CHUNK, g, r), axis=2)
    causal = jnp.tril(jnp.ones((CHUNK, CHUNK), dtype=bool))
    seg = acs[:, :, :, None] - acs[:, :, None, :]
    decay = jnp.exp(jnp.where(causal[None, None, :, :, None, None], seg, -jnp.inf))
    scores = jnp.einsum('bclgn,bcsgn->bclsg', qc, kc).astype(F32)
    y_intra = jnp.einsum('bclsgr,bcsgrp->bclgrp', scores[..., None] * decay, vc.astype(F32))
    w_end = jnp.exp(acs[:, :, -1:] - acs)
    chunk_states = jnp.einsum('bcsgn,bcsgrp->bcgrnp', kc.astype(F32), vc.astype(F32) * w_end[..., None])
    chunk_decay = jnp.exp(acs[:, :, -1])

    def step(state, inp):
        st, dec = inp
        return state * dec[..., None, None] + st, state

    final, prev = lax.scan(step, state0.astype(F32),
                           (jnp.moveaxis(chunk_states, 1, 0), jnp.moveaxis(chunk_decay, 1, 0)))
    prev = jnp.moveaxis(prev, 0, 1)
    y_inter = jnp.einsum('bclgn,bcgrnp->bclgrp', qc.astype(F32), prev) * jnp.exp(acs)[..., None]
    y = (y_intra + y_inter).reshape(b, l, g, r, p)
    return y.astype(v.dtype), final


def bidir_prefix_scan(ctx_in, lat_in):
    qc, kc, vfc, vbc, afc, abc = ctx_in
    ql, kl, vfl, vbl, afl, abl = lat_in
    b, _, g, n = qc.shape
    r, p = vfc.shape[-2:]
    zero = jnp.zeros((b, g, r, n, p), F32)

    def rev(t):
        return jnp.flip(t, axis=1)

    yc_f, sc_f = chunked_scan(qc, kc, vfc, afc, zero)
    yl_f, _ = chunked_scan(ql, kl, vfl, afl, sc_f)
    yc_b, sc_b = chunked_scan(rev(qc), rev(kc), rev(vbc), rev(abc), zero)
    yl_b, _ = chunked_scan(rev(ql), rev(kl), rev(vbl), rev(abl), sc_b)
    return yc_f + rev(yc_b), yl_f + rev(yl_b)


def ssd_prepare(xbc, dt_raw, lp):
    xbc = dwconv_silu(xbc, lp['conv_w'], lp['conv_b'])
    xs, bm, cm = split_cols(xbc, (SSD_D_INNER, SSD_GROUPS * SSD_STATE, SSD_GROUPS * SSD_STATE))
    b, l = xs.shape[:2]
    xh = xs.reshape(b, l, SSD_GROUPS, SSD_HPG, SSD_HEAD_DIM)
    dt = jax.nn.softplus(dt_raw.astype(F32).reshape(b, l, 2, SSD_GROUPS, SSD_HPG)
                         + lp['ssd_dt_bias'].astype(F32).reshape(2, SSD_GROUPS, SSD_HPG))
    log_a = -jnp.exp(lp['ssd_a_log'].astype(F32)).reshape(2, SSD_GROUPS, SSD_HPG) * dt
    v = (xh[:, :, None].astype(F32) * dt[..., None]).astype(xs.dtype)
    q = cm.reshape(b, l, SSD_GROUPS, SSD_STATE)
    k = bm.reshape(b, l, SSD_GROUPS, SSD_STATE)
    return (q, k, v[:, :, 0], v[:, :, 1], log_a[:, :, 0], log_a[:, :, 1]), xh


def ssd_finalize(y, xh, z, lp):
    b, l = z.shape[:2]
    y = y.astype(F32) + lp['ssd_d'].astype(F32).reshape(SSD_GROUPS, SSD_HPG)[:, :, None] * xh.astype(F32)
    y = y.reshape(b, l, SSD_D_INNER) * jax.nn.silu(z.astype(F32))
    yg = y.reshape(b, l, SSD_GROUPS, -1)
    yg = yg * lax.rsqrt(jnp.mean(yg * yg, axis=-1, keepdims=True) + NORM_EPS)
    return (yg.reshape(b, l, SSD_D_INNER) * lp['ssd_norm_g'].astype(F32)).astype(z.dtype)


def ret_prepare(q, k, v, lp, rope):
    b, l = q.shape[:2]
    q = q.reshape(b, l, RET_HEADS, RET_QK_DIM)
    k = k.reshape(b, l, RET_HEADS, RET_QK_DIM)
    if rope is not None:
        q = apply_rope(q, *rope)
        k = apply_rope(k, *rope)
    k = k * (RET_QK_DIM ** -0.5)
    v = v.reshape(b, l, RET_HEADS, 1, RET_V_DIM)
    dec = lp['ret_log_decay'].astype(F32)
    a_f = jnp.broadcast_to(dec[0][:, None], (b, l, RET_HEADS, 1))
    a_b = jnp.broadcast_to(dec[1][:, None], (b, l, RET_HEADS, 1))
    return (q, k, v, v, a_f, a_b)


def ret_finalize(y, g):
    b, l = g.shape[:2]
    yf = y[:, :, :, 0].astype(F32)
    yf = yf - jnp.mean(yf, axis=-1, keepdims=True)
    yf = yf * lax.rsqrt(jnp.mean(yf * yf, axis=-1, keepdims=True) + NORM_EPS)
    return (yf.reshape(b, l, RET_V) * jax.nn.silu(g.astype(F32))).astype(g.dtype)


def split_heads(t, n_heads):
    return t.reshape(t.shape[0], t.shape[1], n_heads, ATTN_HEAD_DIM)


def window_attention(q, k, v, k_ctx, v_ctx, sink):
    b, s, hq, d = q.shape
    hk = k.shape[2]
    grp = hq // hk
    blk = CHUNK
    nb = s // blk
    qb = q.reshape(b, nb, blk, hk, grp, d)

    def band(t):
        tp = jnp.pad(t, ((0, 0), (blk, blk), (0, 0), (0, 0))).reshape(b, nb + 2, blk, hk, d)
        return jnp.concatenate([tp[:, :-2], tp[:, 1:-1], tp[:, 2:]], axis=2)

    kb, vb = band(k), band(v)
    scale = d ** -0.5
    s_win = jnp.einsum('bnqhgd,bnkhd->bnhgqk', qb, kb).astype(F32) * scale
    s_ctx = jnp.einsum('bnqhgd,bkhd->bnhgqk', qb, k_ctx).astype(F32) * scale
    qi = jnp.arange(blk)[:, None]
    kj = jnp.arange(3 * blk)[None, :]
    in_window = jnp.abs(kj - blk - qi) <= WINDOW
    kpos = (jnp.arange(nb)[:, None] - 1) * blk + jnp.arange(3 * blk)[None, :]
    in_seq = (kpos >= 0) & (kpos < s)
    valid = in_window[None] & in_seq[:, None, :]
    s_win = jnp.where(valid[None, :, None, None], s_win, NEG_INF)
    sink_l = jnp.broadcast_to(sink.astype(F32).reshape(hk, grp)[None, None, :, :, None, None],
                              s_win.shape[:-1] + (1,))
    probs = jax.nn.softmax(jnp.concatenate([sink_l, s_ctx, s_win], axis=-1), axis=-1)
    n_ctx = k_ctx.shape[1]
    p_ctx = probs[..., 1:1 + n_ctx].astype(v.dtype)
    p_win = probs[..., 1 + n_ctx:].astype(v.dtype)
    o = (jnp.einsum('bnhgqk,bkhd->bnqhgd', p_ctx, v_ctx)
         + jnp.einsum('bnhgqk,bnkhd->bnqhgd', p_win, vb))
    return o.reshape(b, s, hq * d)


def context_attention(q, k, v, sink):
    b, c, hq, d = q.shape
    hk = k.shape[2]
    grp = hq // hk
    qg = q.reshape(b, c, hk, grp, d)
    s = jnp.einsum('bqhgd,bkhd->bhgqk', qg, k).astype(F32) * (d ** -0.5)
    sink_c = jnp.broadcast_to(sink.astype(F32).reshape(hk, grp)[None, :, :, None, None], s.shape[:-1] + (1,))
    probs = jax.nn.softmax(jnp.concatenate([sink_c, s], axis=-1), axis=-1)
    o = jnp.einsum('bhgqk,bkhd->bqhgd', probs[..., 1:].astype(v.dtype), v)
    return o.reshape(b, c, hq * d)


def merge_branches(ya, yb, yc, gd, lp):
    b, l = gd.shape[:2]
    gates = jax.nn.sigmoid((gd @ lp['w_gate_up']).astype(F32)).reshape(b, l, N_BRANCH, D_MODEL)
    merged = (gates[:, :, 0] * (ya @ lp['w_branch_a']).astype(F32)
              + gates[:, :, 1] * (yb @ lp['w_branch_b']).astype(F32)
              + gates[:, :, 2] * (yc @ lp['w_branch_c']).astype(F32))
    return merged.astype(ya.dtype) @ lp['w_out']


def mixer(hc, hl, lp, rope, emit_ctx):
    n_ctx = hc.shape[1]
    proj = jnp.concatenate([hc, hl], axis=1) @ lp['w_in']
    z_c, xbc_c, dt_c, rq_c, rk_c, rv_c, rg_c, aq_c, ak_c, av_c, gd_c = split_cols(proj[:, :n_ctx], IN_SPLITS)
    z_l, xbc_l, dt_l, rq_l, rk_l, rv_l, rg_l, aq_l, ak_l, av_l, gd_l = split_cols(proj[:, n_ctx:], IN_SPLITS)
    ssd_c, xh_c = ssd_prepare(xbc_c, dt_c, lp)
    ssd_l, xh_l = ssd_prepare(xbc_l, dt_l, lp)
    ya_c, ya_l = bidir_prefix_scan(ssd_c, ssd_l)
    yb_c, yb_l = bidir_prefix_scan(ret_prepare(rq_c, rk_c, rv_c, lp, None),
                                   ret_prepare(rq_l, rk_l, rv_l, lp, rope))
    k_ctx = split_heads(ak_c, ATTN_KV_HEADS)
    v_ctx = split_heads(av_c, ATTN_KV_HEADS)
    yc_l = window_attention(apply_rope(split_heads(aq_l, ATTN_HEADS), *rope),
                            apply_rope(split_heads(ak_l, ATTN_KV_HEADS), *rope),
                            split_heads(av_l, ATTN_KV_HEADS), k_ctx, v_ctx, lp['attn_sink'])
    out_l = merge_branches(ssd_finalize(ya_l, xh_l, z_l, lp), ret_finalize(yb_l, rg_l), yc_l, gd_l, lp)
    out_c = None
    if emit_ctx:
        yc_c = context_attention(split_heads(aq_c, ATTN_HEADS), k_ctx, v_ctx, lp['attn_sink'])
        out_c = merge_branches(ssd_finalize(ya_c, xh_c, z_c, lp), ret_finalize(yb_c, rg_c), yc_c, gd_c, lp)
    return out_c, out_l


def hier_moe(h, lp):
    b, l, d = h.shape
    t = h.reshape(b * l, d)
    g_logits = (t @ lp['router_group_w']).astype(F32) + lp['router_group_b'].astype(F32)
    g_prob = jax.nn.softmax(g_logits, axis=-1)
    g_idx = jnp.argmax(g_logits, axis=-1)
    e_logits = ((t @ lp['router_expert_w']).astype(F32) + lp['router_expert_b'].astype(F32)
                ).reshape(-1, N_GROUPS, EXPERTS_PER_GROUP)
    e_in = jnp.take_along_axis(e_logits, g_idx[:, None, None], axis=1)[:, 0]
    top_v, top_i = lax.top_k(e_in, TOP_K)
    w = jax.nn.softmax(top_v, axis=-1) * jnp.take_along_axis(g_prob, g_idx[:, None], axis=1)
    eid = g_idx[:, None] * EXPERTS_PER_GROUP + top_i
    gates = jnp.sum(jax.nn.one_hot(eid, N_EXPERTS, dtype=F32) * w[..., None], axis=1)
    a = jnp.einsum('td,edf->tef', t, lp['moe_w1'])
    u = jnp.einsum('td,edf->tef', t, lp['moe_w3'])
    hid = (jax.nn.silu(a.astype(F32)) * u.astype(F32) * gates[..., None]).astype(t.dtype)
    y = jnp.einsum('tef,efd->td', hid, lp['moe_w2'])
    return y.reshape(b, l, d)


def trunk_layer(xc, xl, mod_c, mod_l, lp, rope, emit_ctx):
    ml = jnp.split(mod_l[:, None, :], 6, axis=-1)
    mc = jnp.split(mod_c, 6, axis=-1)
    hl = modulate(xl, lp['norm1_g'], ml[0], ml[1])
    hc = modulate(xc, lp['norm1_g'], mc[0], mc[1])
    out_c, out_l = mixer(hc, hl, lp, rope, emit_ctx)
    xl = xl + (ml[2] * out_l).astype(xl.dtype)
    xl = xl + (ml[5] * hier_moe(modulate(xl, lp['norm2_g'], ml[3], ml[4]), lp)).astype(xl.dtype)
    if emit_ctx:
        xc = xc + (mc[2] * out_c).astype(xc.dtype)
        xc = xc + (mc[5] * hier_moe(modulate(xc, lp['norm2_g'], mc[3], mc[4]), lp)).astype(xc.dtype)
    return xc, xl


def setup_inputs(seed: int = 0) -> dict:
    key = jax.random.key(seed)
    ks = jax.random.split(key, 32)
    L, D = DEPTH, D_MODEL

    def nrm(k, shape, scale):
        return jax.random.normal(k, shape, F32) * scale

    a_log = jnp.log(jax.random.uniform(ks[10], (L, 2, SSD_HEADS), F32, 1.0, 16.0))
    dt0 = jnp.exp(jax.random.uniform(ks[11], (L, 2, SSD_HEADS), F32, math.log(1e-3), math.log(1e-1)))
    dt_bias = dt0 + jnp.log(-jnp.expm1(-dt0))
    ret_base = jnp.log(1.0 - 2.0 ** (-5.0 - jnp.arange(RET_HEADS, dtype=F32)))
    ret_log_decay = ret_base[None, None, :] * (1.0 + nrm(ks[14], (L, 2, RET_HEADS), 0.05))
    return {
        'x': nrm(ks[0], (BATCH, SEQ, D), 1.0),
        'c': nrm(ks[1], (BATCH, D), 1.0),
        'ctx': nrm(ks[2], (BATCH, CTX_LEN, D), 1.0),
        'c_ctx': nrm(ks[3], (D,), 1.0),
        'norm1_g': 1.0 + nrm(ks[4], (L, D), 0.02),
        'norm2_g': 1.0 + nrm(ks[5], (L, D), 0.02),
        'mod_down': nrm(ks[6], (L, D, MOD_RANK), D ** -0.5),
        'mod_up': nrm(ks[7], (L, MOD_RANK, 6 * D), 0.5 * MOD_RANK ** -0.5),
        'mod_b': nrm(ks[8], (L, 6 * D), 0.02),
        'w_in': nrm(ks[9], (L, D, IN_WIDTH), D ** -0.5),
        'conv_w': nrm(ks[12], (L, SSD_CONV, SSD_XBC), SSD_CONV ** -0.5),
        'conv_b': nrm(ks[13], (L, SSD_XBC), 0.02),
        'ssd_a_log': a_log,
        'ssd_dt_bias': dt_bias,
        'ssd_d': 1.0 + nrm(ks[15], (L, SSD_HEADS), 0.02),
        'ssd_norm_g': 1.0 + nrm(ks[16], (L, SSD_D_INNER), 0.02),
        'ret_log_decay': ret_log_decay,
        'attn_sink': nrm(ks[17], (L, ATTN_HEADS), 0.5),
        'w_branch_a': nrm(ks[18], (L, SSD_D_INNER, D), SSD_D_INNER ** -0.5),
        'w_branch_b': nrm(ks[19], (L, RET_V, D), RET_V ** -0.5),
        'w_branch_c': nrm(ks[20], (L, ATTN_Q, D), ATTN_Q ** -0.5),
        'w_gate_up': nrm(ks[21], (L, GATE_RANK, N_BRANCH * D), GATE_RANK ** -0.5),
        'w_out': nrm(ks[22], (L, D, D), D ** -0.5),
        'router_group_w': nrm(ks[23], (L, D, N_GROUPS), D ** -0.5),
        'router_group_b': nrm(ks[24], (L, N_GROUPS), 0.01),
        'router_expert_w': nrm(ks[25], (L, D, N_EXPERTS), D ** -0.5),
        'router_expert_b': nrm(ks[26], (L, N_EXPERTS), 0.01),
        'moe_w1': nrm(ks[27], (L, N_EXPERTS, D, EXPERT_FF), D ** -0.5),
        'moe_w3': nrm(ks[28], (L, N_EXPERTS, D, EXPERT_FF), D ** -0.5),
        'moe_w2': nrm(ks[29], (L, N_EXPERTS, EXPERT_FF, D), EXPERT_FF ** -0.5),
        'final_norm_g': 1.0 + nrm(ks[30], (D,), 0.02),
    }


def reference(x, c, ctx, c_ctx, norm1_g, norm2_g, mod_down, mod_up, mod_b, w_in, conv_w, conv_b,
              ssd_a_log, ssd_dt_bias, ssd_d, ssd_norm_g, ret_log_decay, attn_sink,
              w_branch_a, w_branch_b, w_branch_c, w_gate_up, w_out,
              router_group_w, router_group_b, router_expert_w, router_expert_b,
              moe_w1, moe_w3, moe_w2, final_norm_g):
    seq = x.shape[1]
    rope = axial_rope_tables(seq)
    silu_c = jax.nn.silu(c)
    silu_cc = jax.nn.silu(c_ctx)
    xl, xc = x, ctx
    for i in range(DEPTH):
        lp = {
            'norm1_g': norm1_g[i], 'norm2_g': norm2_g[i], 'w_in': w_in[i],
            'conv_w': conv_w[i], 'conv_b': conv_b[i],
            'ssd_a_log': ssd_a_log[i], 'ssd_dt_bias': ssd_dt_bias[i], 'ssd_d': ssd_d[i],
            'ssd_norm_g': ssd_norm_g[i], 'ret_log_decay': ret_log_decay[i], 'attn_sink': attn_sink[i],
            'w_branch_a': w_branch_a[i], 'w_branch_b': w_branch_b[i], 'w_branch_c': w_branch_c[i],
            'w_gate_up': w_gate_up[i], 'w_out': w_out[i],
            'router_group_w': router_group_w[i], 'router_group_b': router_group_b[i],
            'router_expert_w': router_expert_w[i], 'router_expert_b': router_expert_b[i],
            'moe_w1': moe_w1[i], 'moe_w3': moe_w3[i], 'moe_w2': moe_w2[i],
        }
        mod_l = (silu_c @ mod_down[i]) @ mod_up[i] + mod_b[i]
        mod_c = (silu_cc @ mod_down[i]) @ mod_up[i] + mod_b[i]
        xc, xl = trunk_layer(xc, xl, mod_c, mod_l, lp, rope, i < DEPTH - 1)
    return rmsnorm(xl, final_norm_g)
```

```python
import functools
import math

import jax
import jax.numpy as jnp
from jax import lax
from jax.experimental import pallas as pl
from jax.experimental.pallas import tpu as pltpu

F32 = jnp.float32
BF16 = jnp.bfloat16
I32 = jnp.int32

D_MODEL = 4096
DEPTH = 4
GRID_W = 64
CHUNK = 128
NORM_EPS = 1e-6
NEG_INF = -1e30
MOD_RANK = D_MODEL // 16
SSD_D_INNER = D_MODEL // 2
SSD_HEAD_DIM = 64
SSD_HEADS = SSD_D_INNER // SSD_HEAD_DIM
SSD_GROUPS = 4
SSD_HPG = SSD_HEADS // SSD_GROUPS
SSD_STATE = 128
SSD_CONV = 5
SSD_BC = SSD_GROUPS * SSD_STATE
RET_HEADS = 8
RET_QK_DIM = 128
RET_V_DIM = 2 * RET_QK_DIM
ATTN_HEADS = 16
ATTN_KV_HEADS = 4
ATTN_HEAD_DIM = 128
ATTN_GRP = ATTN_HEADS // ATTN_KV_HEADS
WINDOW = 128
ROPE_DIM = 128
ROPE_BASE = 10000.0
N_BRANCH = 3
GATE_RANK = D_MODEL // 16
N_GROUPS = 8
EXPERTS_PER_GROUP = 8
N_EXPERTS = N_GROUPS * EXPERTS_PER_GROUP
EXPERT_FF = D_MODEL // 32
SSD_XBC = SSD_D_INNER + 2 * SSD_BC
RET_Q = RET_HEADS * RET_QK_DIM
RET_V = RET_HEADS * RET_V_DIM
ATTN_Q = ATTN_HEADS * ATTN_HEAD_DIM
ATTN_KV = ATTN_KV_HEADS * ATTN_HEAD_DIM
IN_SPLITS = (SSD_D_INNER, SSD_XBC, 2 * SSD_HEADS, RET_Q, RET_Q, RET_V, RET_V, ATTN_Q, ATTN_KV, ATTN_KV, GATE_RANK)

LANES = 128
BF16_SUBLANES = 16
VMEM_LIMIT = 56 * 1024 * 1024

COL_AQ = 0
COL_RQ = COL_AQ + ATTN_Q
COL_RK = COL_RQ + RET_Q
COL_AK = COL_RK + RET_Q
N_ROPE_COLS = COL_AK + ATTN_KV
COL_AV = N_ROPE_COLS
COL_B = COL_AV + ATTN_KV
COL_C = COL_B + SSD_BC
COL_Z = COL_C + SSD_BC
COL_XS = COL_Z + SSD_D_INNER
COL_RV = COL_XS + SSD_D_INNER
COL_RG = COL_RV + RET_V
COL_GD = COL_RG + RET_V
PROJ_W = COL_GD + GATE_RANK
WIN_TN = 768
DT_W = 2 * LANES

MOE_TM = 256
MOE_SPLIT = 4
MOE_SLAB = EXPERTS_PER_GROUP * EXPERT_FF // MOE_SPLIT


def _cparams(sem):
    return pltpu.CompilerParams(dimension_semantics=sem, vmem_limit_bytes=VMEM_LIMIT)


def _silu(x):
    return x * (1.0 / (1.0 + jnp.exp(-x)))


def _split3(x):
    hi = x.astype(BF16)
    r1 = x - hi.astype(F32)
    mid = r1.astype(BF16)
    lo = (r1 - mid.astype(F32)).astype(BF16)
    return hi, mid, lo


def _dot01_left(m01, x):
    hi, mid, lo = _split3(x)
    d = lambda t: jnp.dot(m01, t, preferred_element_type=F32)
    return d(hi) + d(mid) + d(lo)


def _dot01_right(x, m01):
    hi, mid, lo = _split3(x)
    d = lambda t: jnp.dot(t, m01, preferred_element_type=F32)
    return d(hi) + d(mid) + d(lo)


def _mod_body(c_ref, down_ref, up_ref, b_ref, o_ref):
    s = _silu(c_ref[...])
    r = jnp.dot(s.astype(BF16), down_ref[0].astype(BF16), preferred_element_type=F32)
    o_ref[0] = jnp.dot(r.astype(BF16), up_ref[0].astype(BF16), preferred_element_type=F32) + b_ref[0]


def _modulation(cc, mod_down, mod_up, mod_b):
    depth = mod_down.shape[0]
    d = cc.shape[1]
    n_t = 6
    return pl.pallas_call(
        _mod_body,
        grid=(depth, n_t),
        in_specs=[
            pl.BlockSpec((16, d), lambda l, j: (0, 0)),
            pl.BlockSpec((1, d, MOD_RANK), lambda l, j: (l, 0, 0)),
            pl.BlockSpec((1, MOD_RANK, d), lambda l, j: (l, 0, j)),
            pl.BlockSpec((1, 1, d), lambda l, j: (l, 0, j)),
        ],
        out_specs=pl.BlockSpec((1, 16, d), lambda l, j: (l, 0, j)),
        out_shape=jax.ShapeDtypeStruct((depth, 16, 6 * d), F32),
        compiler_params=_cparams(("arbitrary", "arbitrary")),
        name="modulation",
    )(cc, mod_down, mod_up, mod_b.reshape(depth, 1, 6 * d))


def _mod_rows(mod_ref, k, is_ctx):
    d = D_MODEL
    return jnp.where(is_ctx, mod_ref[0:1, k * d:(k + 1) * d], mod_ref[1:2, k * d:(k + 1) * d])


def _normmod_body(x_ref, g_ref, mod_ref, o_ref, *, n_ctx, tm, k_shift, k_scale):
    x = x_ref[...]
    y = x * lax.rsqrt(jnp.mean(x * x, axis=-1, keepdims=True) + NORM_EPS) * g_ref[...]
    row = pl.program_id(0) * tm + lax.broadcasted_iota(I32, (tm, 1), 0)
    is_ctx = row < n_ctx
    o = y * (1.0 + _mod_rows(mod_ref, k_scale, is_ctx)) + _mod_rows(mod_ref, k_shift, is_ctx)
    o_ref[...] = o.astype(o_ref.dtype)


def _normmod(x, g, mod, *, n_ctx, k_shift, k_scale, out_dtype, tm=256):
    t, d = x.shape
    return pl.pallas_call(
        functools.partial(_normmod_body, n_ctx=n_ctx, tm=tm, k_shift=k_shift, k_scale=k_scale),
        grid=(t // tm,),
        in_specs=[
            pl.BlockSpec((tm, d), lambda i: (i, 0)),
            pl.BlockSpec((1, d), lambda i: (0, 0)),
            pl.BlockSpec((16, 6 * d), lambda i: (0, 0)),
        ],
        out_specs=pl.BlockSpec((tm, d), lambda i: (i, 0)),
        out_shape=jax.ShapeDtypeStruct((t, d), out_dtype),
        compiler_params=_cparams(("parallel",)),
        name="normmod",
    )(x, g.reshape(1, d), mod)


def _rope_tile(acc, cos, sin_signed, first_half):
    rot = jnp.where(first_half, pltpu.roll(acc, ROPE_DIM - ROPE_DIM // 4, 1), pltpu.roll(acc, ROPE_DIM // 4, 1))
    return acc * cos + rot * sin_signed


def _win_body(x_ref, w_ref, cos_ref, sin_ref, cs_ref, o_ref, *, n_rope_tiles, tm, tn):
    acc = jnp.dot(x_ref[...], w_ref[...], preferred_element_type=F32)
    j = pl.program_id(1)

    @pl.when(j < n_rope_tiles)
    def _():
        cos = cos_ref[...]
        sin = sin_ref[...]
        lane = lax.broadcasted_iota(I32, (tm, LANES), 1)
        first_half = (lane % (ROPE_DIM // 2)) < (ROPE_DIM // 4)
        for h in range(tn // LANES):
            sl = slice(h * LANES, (h + 1) * LANES)
            o_ref[:, sl] = (_rope_tile(acc[:, sl], cos, sin, first_half) * cs_ref[:, sl]).astype(o_ref.dtype)

    @pl.when(j >= n_rope_tiles)
    def _():
        o_ref[...] = acc.astype(o_ref.dtype)


def _in_proj(h, w, cos, sin, colscale, *, tm):
    t, k = h.shape
    n = w.shape[1]
    tn = WIN_TN
    return pl.pallas_call(
        functools.partial(_win_body, n_rope_tiles=N_ROPE_COLS // tn, tm=tm, tn=tn),
        grid=(t // tm, n // tn),
        in_specs=[
            pl.BlockSpec((tm, k), lambda i, j: (i, 0)),
            pl.BlockSpec((k, tn), lambda i, j: (0, j)),
            pl.BlockSpec((tm, LANES), lambda i, j: (i, 0)),
            pl.BlockSpec((tm, LANES), lambda i, j: (i, 0)),
            pl.BlockSpec((1, tn), lambda i, j: (0, jnp.minimum(j, N_ROPE_COLS // tn - 1))),
        ],
        out_specs=pl.BlockSpec((tm, tn), lambda i, j: (i, j)),
        out_shape=jax.ShapeDtypeStruct((t, n), BF16),
        compiler_params=_cparams(("parallel", "arbitrary")),
        name="in_proj",
    )(h, w, cos, sin, colscale)


def _mm_body(x_ref, w_ref, o_ref):
    o_ref[...] = jnp.dot(x_ref[...], w_ref[...], preferred_element_type=F32).astype(o_ref.dtype)


def _matmul(x, w, *, tm, tn, out_dtype, name):
    t, k = x.shape
    n = w.shape[1]
    return pl.pallas_call(
        _mm_body,
        grid=(t // tm, n // tn),
        in_specs=[pl.BlockSpec((tm, k), lambda i, j: (i, 0)), pl.BlockSpec((k, tn), lambda i, j: (0, j))],
        out_specs=pl.BlockSpec((tm, tn), lambda i, j: (i, j)),
        out_shape=jax.ShapeDtypeStruct((t, n), out_dtype),
        compiler_params=_cparams(("parallel", "arbitrary")),
        name=name,
    )(x, w)


def _chunk_of_step(i, *, backward, n_ctx_chunks, n_chunks):
    if not backward:
        return i
    return jnp.where(i < n_ctx_chunks, n_ctx_chunks - 1 - i, n_chunks - 1 - (i - n_ctx_chunks))


def _tri(backward):
    l = lax.broadcasted_iota(I32, (CHUNK, CHUNK), 0)
    s = lax.broadcasted_iota(I32, (CHUNK, CHUNK), 1)
    return (s >= l) if backward else (s <= l)


def _conv_silu(ext_ref, prev_ref, cur_ref, next_ref, w_ref, b_ref, zero_prev, zero_next):
    half = SSD_CONV // 2
    pad = 8
    hp = prev_ref[...].astype(F32)[BF16_SUBLANES - pad:, :]
    hn = next_ref[...].astype(F32)[:pad, :]
    ext_ref[0:pad, :] = hp * jnp.where(zero_prev, 0.0, 1.0)
    ext_ref[pad:pad + CHUNK, :] = cur_ref[...].astype(F32)
    ext_ref[pad + CHUNK:, :] = hn * jnp.where(zero_next, 0.0, 1.0)
    acc = None
    for j in range(SSD_CONV):
        term = ext_ref[pad - half + j:pad - half + j + CHUNK, :] * w_ref[j:j + 1, :]
        acc = term if acc is None else acc + term
    return _silu(acc + b_ref[...])


def _ssd_body(*refs, backward, finalize, n_ctx_chunks, n_chunks):
    if finalize:
        (xs_ref, b_ref, c_ref, dt_ref, dtb_ref, alog_ref, e64_ref, yf_ref, z_ref, dx_ref, ng_ref,
         y_ref, state_ref) = refs
    else:
        (xs_p, xs_c, xs_n, b_p, b_c, b_n, c_p, c_c, c_n, wx_ref, wb_ref, wc_ref, bx_ref, bb_ref, bc_ref,
         dt_ref, dtb_ref, alog_ref, e64_ref,
         y_ref, xs_o, b_o, c_o, state_ref, extx_ref, extb_ref) = refs
    i = pl.program_id(0)
    chunk = _chunk_of_step(i, backward=backward, n_ctx_chunks=n_ctx_chunks, n_chunks=n_chunks)

    @pl.when(i == 0)
    def _():
        state_ref[...] = jnp.zeros_like(state_ref)

    if finalize:
        xs = xs_ref[...].astype(F32)
        bm = b_ref[...]
        cm = c_ref[...]
    else:
        zero_prev = (chunk == 0) | (chunk == n_ctx_chunks)
        zero_next = (chunk == n_ctx_chunks - 1) | (chunk == n_chunks - 1)
        xs = _conv_silu(extx_ref, xs_p, xs_c, xs_n, wx_ref, bx_ref, zero_prev, zero_next)
        bm = _conv_silu(extb_ref, b_p, b_c, b_n, wb_ref, bb_ref, zero_prev, zero_next).astype(BF16)
        cm = _conv_silu(extb_ref, c_p, c_c, c_n, wc_ref, bc_ref, zero_prev, zero_next).astype(BF16)
        xs_o[...] = xs.astype(BF16)
        b_o[...] = bm
        c_o[...] = cm
    xs_b = xs.astype(BF16)

    x = dt_ref[...] + dtb_ref[...]
    dtp = jnp.maximum(x, 0.0) + jnp.log(1.0 + jnp.exp(-jnp.abs(x)))
    la = -jnp.exp(alog_ref[...]) * dtp
    mask = _tri(backward)
    tri01 = jnp.where(mask, 1.0, 0.0).astype(BF16)
    acs = _dot01_left(tri01, la)
    total = acs[0:1, :] if backward else acs[CHUNK - 1:CHUNK, :]
    acs_t = acs.T
    dt_t = dtp.T
    wd_t = (jnp.exp(total - acs) * dtp).T
    cdec = jnp.broadcast_to(jnp.exp(total), (8, LANES))
    cdec_x = _dot01_right(cdec, e64_ref[...])[0:1, :]
    lane = lax.broadcasted_iota(I32, (CHUNK, LANES), 1)
    low = lane < SSD_HEAD_DIM

    y_pairs = []
    for g in range(SSD_GROUPS):
        q_g = cm[:, g * SSD_STATE:(g + 1) * SSD_STATE]
        k_g = bm[:, g * SSD_STATE:(g + 1) * SSD_STATE]
        scores = lax.dot_general(q_g, k_g, (((1,), (1,)), ((), ())), preferred_element_type=F32)
        k_t = k_g.astype(F32).T
        for pr in range(SSD_HPG // 2):
            ha = g * SSD_HPG + 2 * pr
            hb = ha + 1
            pair = ha // 2
            sl = slice(pair * LANES, (pair + 1) * LANES)
            col_a = jnp.broadcast_to(acs[:, ha:ha + 1], (CHUNK, CHUNK))
            col_b = jnp.broadcast_to(acs[:, hb:hb + 1], (CHUNK, CHUNK))
            dec_a = jnp.exp(jnp.where(mask, col_a - acs_t[ha:ha + 1, :], -jnp.inf))
            dec_b = jnp.exp(jnp.where(mask, col_b - acs_t[hb:hb + 1, :], -jnp.inf))
            m_a = scores * dec_a * dt_t[ha:ha + 1, :]
            m_b = scores * dec_b * dt_t[hb:hb + 1, :]
            s_a = k_t * wd_t[ha:ha + 1, :]
            s_b = k_t * wd_t[hb:hb + 1, :]
            lhs = jnp.concatenate(
                [jnp.concatenate([m_a, m_b], axis=1), jnp.concatenate([s_a, s_b], axis=1)], axis=0).astype(BF16)
            xp = xs_b[:, sl]
            zero = jnp.zeros_like(xp)
            rhs = jnp.concatenate([jnp.where(low, xp, zero), jnp.where(low, zero, xp)], axis=0)
            both = jnp.dot(lhs, rhs, preferred_element_type=F32)
            prev = state_ref[pair]
            inter = jnp.dot(q_g, prev.astype(BF16), preferred_element_type=F32)
            inter = inter * jnp.exp(jnp.where(low, col_a, col_b))
            y_pair = both[:CHUNK] + inter
            state_ref[pair] = prev * cdec_x[:, sl] + both[CHUNK:]
            if finalize:
                y_pairs.append(y_pair + yf_ref[:, sl] + dx_ref[:, sl] * xs[:, sl])
            else:
                y_ref[:, sl] = y_pair

    if finalize:
        y = jnp.concatenate(y_pairs, axis=1) * _silu(z_ref[...].astype(F32))
        gw = SSD_D_INNER // SSD_GROUPS
        for g in range(SSD_GROUPS):
            sl = slice(g * gw, (g + 1) * gw)
            yg = y[:, sl]
            yg = yg * lax.rsqrt(jnp.mean(yg * yg, axis=-1, keepdims=True) + NORM_EPS)
            y_ref[:, sl] = (yg * ng_ref[:, sl]).astype(y_ref.dtype)


def _ssd_forward(proj, dtraw, conv_wx, conv_wb, conv_wc, conv_bx, conv_bb, conv_bc, dtb, alog, e64, *, n_ctx):
    t = proj.shape[0]
    n_chunks = t // CHUNK
    ncc = n_ctx // CHUNK
    hb = CHUNK // BF16_SUBLANES
    n_hblk = t // BF16_SUBLANES
    cmap = lambda i: i

    def cur(col, w):
        return pl.BlockSpec((CHUNK, w), lambda i: (cmap(i), col // w))

    def prev(col, w):
        return pl.BlockSpec((BF16_SUBLANES, w), lambda i: (jnp.maximum(cmap(i) * hb - 1, 0), col // w))

    def nxt(col, w):
        return pl.BlockSpec((BF16_SUBLANES, w), lambda i: (jnp.minimum((cmap(i) + 1) * hb, n_hblk - 1), col // w))

    const = lambda r, c: pl.BlockSpec((r, c), lambda i: (0, 0))
    in_specs = [
        prev(COL_XS, SSD_D_INNER), cur(COL_XS, SSD_D_INNER), nxt(COL_XS, SSD_D_INNER),
        prev(COL_B, SSD_BC), cur(COL_B, SSD_BC), nxt(COL_B, SSD_BC),
        prev(COL_C, SSD_BC), cur(COL_C, SSD_BC), nxt(COL_C, SSD_BC),
        const(8, SSD_D_INNER), const(8, SSD_BC), const(8, SSD_BC),
        const(1, SSD_D_INNER), const(1, SSD_BC), const(1, SSD_BC),
        pl.BlockSpec((CHUNK, LANES), lambda i: (cmap(i), 0)),
        const(1, LANES), const(1, LANES), const(LANES, SSD_D_INNER),
    ]
    out_specs = [
        pl.BlockSpec((CHUNK, SSD_D_INNER), lambda i: (cmap(i), 0)),
        pl.BlockSpec((CHUNK, SSD_D_INNER), lambda i: (cmap(i), 0)),
        pl.BlockSpec((CHUNK, SSD_BC), lambda i: (cmap(i), 0)),
        pl.BlockSpec((CHUNK, SSD_BC), lambda i: (cmap(i), 0)),
    ]
    out_shape = [
        jax.ShapeDtypeStruct((t, SSD_D_INNER), F32),
        jax.ShapeDtypeStruct((t, SSD_D_INNER), BF16),
        jax.ShapeDtypeStruct((t, SSD_BC), BF16),
        jax.ShapeDtypeStruct((t, SSD_BC), BF16),
    ]
    return pl.pallas_call(
        functools.partial(_ssd_body, backward=False, finalize=False, n_ctx_chunks=ncc, n_chunks=n_chunks),
        grid=(n_chunks,),
        in_specs=in_specs,
        out_specs=out_specs,
        out_shape=out_shape,
        scratch_shapes=[
            pltpu.VMEM((SSD_HEADS // 2, SSD_STATE, LANES), F32),
            pltpu.VMEM((CHUNK + 16, SSD_D_INNER), F32),
            pltpu.VMEM((CHUNK + 16, SSD_BC), F32),
        ],
        compiler_params=_cparams(("arbitrary",)),
        name="ssd_fwd",
    )(proj, proj, proj, proj, proj, proj, proj, proj, proj,
      conv_wx, conv_wb, conv_wc, conv_bx, conv_bb, conv_bc, dtraw, dtb[0:1], alog[0:1], e64)


def _ssd_backward(proj, xs, bm, cm, dtraw, dtb, alog, e64, yf, d_x, norm_g, *, n_ctx):
    t = proj.shape[0]
    n_chunks = t // CHUNK
    ncc = n_ctx // CHUNK
    cmap = functools.partial(_chunk_of_step, backward=True, n_ctx_chunks=ncc, n_chunks=n_chunks)
    row = lambda w, col=0: pl.BlockSpec((CHUNK, w), lambda i: (cmap(i), col // w))
    const = lambda r, c: pl.BlockSpec((r, c), lambda i: (0, 0))
    in_specs = [
        row(SSD_D_INNER), row(SSD_BC), row(SSD_BC),
        pl.BlockSpec((CHUNK, LANES), lambda i: (cmap(i), 1)),
        const(1, LANES), const(1, LANES), const(LANES, SSD_D_INNER),
        row(SSD_D_INNER), row(SSD_D_INNER, COL_Z), const(1, SSD_D_INNER), const(1, SSD_D_INNER),
    ]
    return pl.pallas_call(
        functools.partial(_ssd_body, backward=True, finalize=True, n_ctx_chunks=ncc, n_chunks=n_chunks),
        grid=(n_chunks,),
        in_specs=in_specs,
        out_specs=row(SSD_D_INNER),
        out_shape=jax.ShapeDtypeStruct((t, SSD_D_INNER), BF16),
        scratch_shapes=[pltpu.VMEM((SSD_HEADS // 2, SSD_STATE, LANES), F32)],
        compiler_params=_cparams(("arbitrary",)),
        name="ssd_bwd",
    )(xs, bm, cm, dtraw, dtb[1:2], alog[1:2], e64, yf, proj, d_x, norm_g)


def _ret_body(*refs, backward, finalize, n_ctx_chunks, n_chunks):
    if finalize:
        dec_ref, q_ref, k_ref, v_ref, yf_ref, g_ref, y_ref, state_ref = refs
    else:
        dec_ref, q_ref, k_ref, v_ref, y_ref, state_ref = refs
    i = pl.program_id(0)

    @pl.when(i == 0)
    def _():
        state_ref[...] = jnp.zeros_like(state_ref)

    mask = _tri(backward)
    l_i = lax.broadcasted_iota(I32, (CHUNK, CHUNK), 0)
    s_i = lax.broadcasted_iota(I32, (CHUNK, CHUNK), 1)
    dist = ((s_i - l_i) if backward else (l_i - s_i)).astype(F32)
    pos = lax.broadcasted_iota(I32, (CHUNK, 1), 0).astype(F32)
    srow = lax.broadcasted_iota(I32, (1, CHUNK), 1).astype(F32)
    steps_in = (CHUNK - pos) if backward else (pos + 1.0)
    steps_out = srow if backward else (CHUNK - 1.0 - srow)
    d = 1 if backward else 0
    for h in range(RET_HEADS):
        a = dec_ref[d, h]
        qh = q_ref[:, h * RET_QK_DIM:(h + 1) * RET_QK_DIM]
        kh = k_ref[:, h * RET_QK_DIM:(h + 1) * RET_QK_DIM]
        vh = v_ref[:, h * RET_V_DIM:(h + 1) * RET_V_DIM]
        scores = lax.dot_general(qh, kh, (((1,), (1,)), ((), ())), preferred_element_type=F32)
        m = scores * jnp.exp(jnp.where(mask, a * dist, -jnp.inf))
        intra = jnp.dot(m.astype(BF16), vh, preferred_element_type=F32)
        prev = state_ref[h]
        inter = jnp.dot(qh, prev.astype(BF16), preferred_element_type=F32) * jnp.exp(a * steps_in)
        kw = (kh.astype(F32).T * jnp.exp(a * steps_out)).astype(BF16)
        carry = jnp.exp(a * jnp.full((1, RET_V_DIM), float(CHUNK), F32))
        state_ref[h] = prev * carry + jnp.dot(kw, vh, preferred_element_type=F32)
        y = intra + inter
        sl = slice(h * RET_V_DIM, (h + 1) * RET_V_DIM)
        if finalize:
            y = y + yf_ref[:, sl]
            y = y - jnp.mean(y, axis=-1, keepdims=True)
            y = y * lax.rsqrt(jnp.mean(y * y, axis=-1, keepdims=True) + NORM_EPS)
            y_ref[:, sl] = (y * _silu(g_ref[:, sl].astype(F32))).astype(y_ref.dtype)
        else:
            y_ref[:, sl] = y


def _retention(proj, log_decay, yf, *, backward, n_ctx):
    t = proj.shape[0]
    n_chunks = t // CHUNK
    ncc = n_ctx // CHUNK
    cmap = functools.partial(_chunk_of_step, backward=backward, n_ctx_chunks=ncc, n_chunks=n_chunks)
    row = lambda w, col=0: pl.BlockSpec((CHUNK, w), lambda i: (cmap(i), col // w))
    in_specs = [pl.BlockSpec(memory_space=pltpu.SMEM), row(RET_Q, COL_RQ), row(RET_Q, COL_RK), row(RET_V, COL_RV)]
    args = [log_decay, proj, proj, proj]
    if backward:
        in_specs += [row(RET_V), row(RET_V, COL_RG)]
        args += [yf, proj]
    return pl.pallas_call(
        functools.partial(_ret_body, backward=backward, finalize=backward, n_ctx_chunks=ncc, n_chunks=n_chunks),
        grid=(n_chunks,),
        in_specs=in_specs,
        out_specs=row(RET_V),
        out_shape=jax.ShapeDtypeStruct((t, RET_V), BF16 if backward else F32),
        scratch_shapes=[pltpu.VMEM((RET_HEADS, RET_QK_DIM, RET_V_DIM), F32)],
        compiler_params=_cparams(("arbitrary",)),
        name="ret_bwd" if backward else "ret_fwd",
    )(*args)


def _attn_body(sink_ref, q_ref, kc_ref, vc_ref, kp_ref, kq_ref, kn_ref, vp_ref, vq_ref, vn_ref, o_ref,
               *, n_ctx_chunks, n_chunks, n_ctx):
    i = pl.program_id(0)
    is_lat = i >= n_ctx_chunks
    rows = ATTN_GRP * CHUNK
    qi = lax.broadcasted_iota(I32, (rows, CHUNK), 0) % CHUNK
    kj = lax.broadcasted_iota(I32, (rows, CHUNK), 1)
    ok_prev = kj >= qi + jnp.where(i > n_ctx_chunks, 0, CHUNK)
    ok_cur = kj >= jnp.where(is_lat, 0, CHUNK)
    ok_next = kj <= qi - jnp.where(is_lat & (i < n_chunks - 1), 0, CHUNK)
    scale = ATTN_HEAD_DIM ** -0.5
    for hk in range(ATTN_KV_HEADS):
        ksl = slice(hk * ATTN_HEAD_DIM, (hk + 1) * ATTN_HEAD_DIM)
        q = jnp.concatenate(
            [q_ref[:, (hk * ATTN_GRP + g) * ATTN_HEAD_DIM:(hk * ATTN_GRP + g + 1) * ATTN_HEAD_DIM]
             for g in range(ATTN_GRP)], axis=0)
        sink = jnp.concatenate(
            [jnp.full((CHUNK, 1), sink_ref[hk * ATTN_GRP + g], F32) for g in range(ATTN_GRP)], axis=0)
        qk = lambda k: lax.dot_general(q, k, (((1,), (1,)), ((), ())), preferred_element_type=F32) * scale
        s_c = qk(kc_ref[:, ksl])
        s_p = jnp.where(ok_prev, qk(kp_ref[:, ksl]), NEG_INF)
        s_q = jnp.where(ok_cur, qk(kq_ref[:, ksl]), NEG_INF)
        s_n = jnp.where(ok_next, qk(kn_ref[:, ksl]), NEG_INF)
        mx = lambda s: jnp.max(s, axis=-1, keepdims=True)
        m = jnp.maximum(jnp.maximum(jnp.maximum(mx(s_c), mx(s_p)), jnp.maximum(mx(s_q), mx(s_n))), sink)
        p_c = jnp.exp(s_c - m)
        p_p = jnp.exp(s_p - m)
        p_q = jnp.exp(s_q - m)
        p_n = jnp.exp(s_n - m)
        sm = lambda p: jnp.sum(p, axis=-1, keepdims=True)
        den = jnp.exp(sink - m) + sm(p_c) + sm(p_p) + sm(p_q) + sm(p_n)
        pv = lambda p, v: jnp.dot(p.astype(BF16), v, preferred_element_type=F32)
        o = (pv(p_c, vc_ref[:, ksl]) + pv(p_p, vp_ref[:, ksl]) + pv(p_q, vq_ref[:, ksl]) + pv(p_n, vn_ref[:, ksl])) / den
        for g in range(ATTN_GRP):
            h = hk * ATTN_GRP + g
            o_ref[:, h * ATTN_HEAD_DIM:(h + 1) * ATTN_HEAD_DIM] = o[g * CHUNK:(g + 1) * CHUNK].astype(o_ref.dtype)


def _attention(proj, sink, *, n_ctx):
    t = proj.shape[0]
    n_chunks = t // CHUNK
    ncc = n_ctx // CHUNK
    kcol = COL_AK // ATTN_KV
    vcol = COL_AV // ATTN_KV
    lat = lambda i, off: jnp.clip(i + off, ncc, n_chunks - 1)
    in_specs = [
        pl.BlockSpec(memory_space=pltpu.SMEM),
        pl.BlockSpec((CHUNK, ATTN_Q), lambda i: (i, COL_AQ // ATTN_Q)),
        pl.BlockSpec((n_ctx, ATTN_KV), lambda i: (0, kcol)),
        pl.BlockSpec((n_ctx, ATTN_KV), lambda i: (0, vcol)),
        pl.BlockSpec((CHUNK, ATTN_KV), lambda i: (lat(i, -1), kcol)),
        pl.BlockSpec((CHUNK, ATTN_KV), lambda i: (i, kcol)),
        pl.BlockSpec((CHUNK, ATTN_KV), lambda i: (lat(i, 1), kcol)),
        pl.BlockSpec((CHUNK, ATTN_KV), lambda i: (lat(i, -1), vcol)),
        pl.BlockSpec((CHUNK, ATTN_KV), lambda i: (i, vcol)),
        pl.BlockSpec((CHUNK, ATTN_KV), lambda i: (lat(i, 1), vcol)),
    ]
    return pl.pallas_call(
        functools.partial(_attn_body, n_ctx_chunks=ncc, n_chunks=n_chunks, n_ctx=n_ctx),
        grid=(n_chunks,),
        in_specs=in_specs,
        out_specs=pl.BlockSpec((CHUNK, ATTN_Q), lambda i: (i, 0)),
        out_shape=jax.ShapeDtypeStruct((t, ATTN_Q), BF16),
        compiler_params=_cparams(("parallel",)),
        name="attention",
    )(sink, proj, proj, proj, proj, proj, proj, proj, proj, proj)


def _merge_body(ya_ref, yb_ref, yc_ref, gd_ref, wa_ref, wb_ref, wc_ref, ga_ref, gb_ref, gc_ref, o_ref):
    gd = gd_ref[...]

    def branch(y_ref, w_ref, g_ref):
        gate = jax.nn.sigmoid(jnp.dot(gd, g_ref[...], preferred_element_type=F32))
        return gate * jnp.dot(y_ref[...], w_ref[...], preferred_element_type=F32)

    o_ref[...] = (branch(ya_ref, wa_ref, ga_ref) + branch(yb_ref, wb_ref, gb_ref)
                  + branch(yc_ref, wc_ref, gc_ref)).astype(o_ref.dtype)


def _merge(ya, yb, yc, proj, wa, wb, wc, wg, *, tm, tn=512):
    t = ya.shape[0]
    d = wa.shape[1]
    nj = d // tn
    act = lambda w: pl.BlockSpec((tm, w), lambda i, j: (i, 0))
    wsp = lambda k: pl.BlockSpec((k, tn), lambda i, j: (0, j))
    gsp = lambda b: pl.BlockSpec((GATE_RANK, tn), lambda i, j: (0, b * nj + j))
    return pl.pallas_call(
        _merge_body,
        grid=(t // tm, nj),
        in_specs=[act(SSD_D_INNER), act(RET_V), act(ATTN_Q),
                  pl.BlockSpec((tm, GATE_RANK), lambda i, j: (i, COL_GD // GATE_RANK)),
                  wsp(SSD_D_INNER), wsp(RET_V), wsp(ATTN_Q), gsp(0), gsp(1), gsp(2)],
        out_specs=pl.BlockSpec((tm, tn), lambda i, j: (i, j)),
        out_shape=jax.ShapeDtypeStruct((t, d), BF16),
        compiler_params=_cparams(("parallel", "arbitrary")),
        name="merge",
    )(ya, yb, yc, proj, wa, wb, wc, wg, wg, wg)


def _outproj_body(m_ref, w_ref, x_ref, mod_ref, o_ref, *, n_ctx, tm, tn, k_gate):
    acc = jnp.dot(m_ref[...], w_ref[...], preferred_element_type=F32)
    row = pl.program_id(0) * tm + lax.broadcasted_iota(I32, (tm, 1), 0)
    gate = jnp.where(row < n_ctx, mod_ref[0:1, :], mod_ref[1:2, :])
    o_ref[...] = x_ref[...] + gate * acc


def _outproj(merged, w, x, mod, *, n_ctx, k_gate, tm, tn=512):
    t, k = merged.shape
    d = w.shape[1]
    nj = d // tn
    return pl.pallas_call(
        functools.partial(_outproj_body, n_ctx=n_ctx, tm=tm, tn=tn, k_gate=k_gate),
        grid=(t // tm, nj),
        in_specs=[
            pl.BlockSpec((tm, k), lambda i, j: (i, 0)),
            pl.BlockSpec((k, tn), lambda i, j: (0, j)),
            pl.BlockSpec((tm, tn), lambda i, j: (i, j)),
            pl.BlockSpec((16, tn), lambda i, j: (0, k_gate * nj + j)),
        ],
        out_specs=pl.BlockSpec((tm, tn), lambda i, j: (i, j)),
        out_shape=jax.ShapeDtypeStruct((t, d), F32),
        compiler_params=_cparams(("parallel", "arbitrary")),
        name="outproj",
    )(merged, w, x, mod)


def _router_logits(hb, wr_ref, rb_ref):
    return jnp.dot(hb, wr_ref[...], preferred_element_type=F32) + rb_ref[...]


def _norm2_body(x_ref, g_ref, mod_ref, wr_ref, rb_ref, h_ref, gi_ref, *, n_ctx, tm):
    x = x_ref[...]
    y = x * lax.rsqrt(jnp.mean(x * x, axis=-1, keepdims=True) + NORM_EPS) * g_ref[...]
    row = pl.program_id(0) * tm + lax.broadcasted_iota(I32, (tm, 1), 0)
    is_ctx = row < n_ctx
    h = y * (1.0 + _mod_rows(mod_ref, 4, is_ctx)) + _mod_rows(mod_ref, 3, is_ctx)
    h_ref[...] = h
    logits = _router_logits(h.astype(BF16), wr_ref, rb_ref)
    lane = lax.broadcasted_iota(I32, (tm, LANES), 1)
    is_g = (lane >= N_EXPERTS) & (lane < N_EXPERTS + N_GROUPS)
    gl = jnp.where(is_g, logits, -jnp.inf)
    gmax = jnp.max(gl, axis=-1, keepdims=True)
    first = jnp.min(jnp.where(gl == gmax, (lane - N_EXPERTS).astype(F32), float(N_GROUPS)), axis=-1, keepdims=True)
    gi_ref[...] = first.astype(I32)


def _norm2_router(x, g, mod, wr, rb, *, n_ctx, tm=256):
    t, d = x.shape
    return pl.pallas_call(
        functools.partial(_norm2_body, n_ctx=n_ctx, tm=tm),
        grid=(t // tm,),
        in_specs=[
            pl.BlockSpec((tm, d), lambda i: (i, 0)),
            pl.BlockSpec((1, d), lambda i: (0, 0)),
            pl.BlockSpec((16, 6 * d), lambda i: (0, 0)),
            pl.BlockSpec((d, LANES), lambda i: (0, 0)),
            pl.BlockSpec((1, LANES), lambda i: (0, 0)),
        ],
        out_specs=[pl.BlockSpec((tm, d), lambda i: (i, 0)), pl.BlockSpec((tm, 1), lambda i: (i, 0))],
        out_shape=[jax.ShapeDtypeStruct((t, d), F32), jax.ShapeDtypeStruct((t, 1), I32)],
        compiler_params=_cparams(("parallel",)),
        name="norm2_router",
    )(x, g.reshape(1, d), mod, wr, rb)


def _positions_body(g_ref, pos_ref, tg_ref, *, n_rows, tm):
    gidx = g_ref[...]
    li = lax.broadcasted_iota(I32, (LANES, LANES), 0)
    lj = lax.broadcasted_iota(I32, (LANES, LANES), 1)
    upper = jnp.where(li <= lj, 1.0, 0.0).astype(BF16)
    ri = lax.broadcasted_iota(I32, (n_rows, n_rows), 0)
    rj = lax.broadcasted_iota(I32, (n_rows, n_rows), 1)
    strict = jnp.where(rj < ri, 1.0, 0.0).astype(BF16)
    tile_start = (lax.broadcasted_iota(I32, (8, LANES), 1) * tm).astype(F32)
    pos = jnp.zeros((n_rows, LANES), F32)
    tg = jnp.zeros((8, LANES), F32)
    off = jnp.zeros((1, 1), F32)
    for g in range(N_GROUPS):
        mk = jnp.where(gidx == g, 1.0, 0.0)
        within = jnp.dot(mk.astype(BF16), upper, preferred_element_type=F32)
        rowtot = jnp.broadcast_to(within[:, LANES - 1:LANES], (n_rows, LANES))
        before = jnp.dot(strict, rowtot.astype(BF16), preferred_element_type=F32)
        pos = pos + mk * (off + before + within - 1.0)
        count = jnp.sum(jnp.sum(mk, axis=-1, keepdims=True), axis=0, keepdims=True)
        padded = jnp.floor((count + (tm - 1.0)) * (1.0 / tm)) * tm
        off = off + padded
        tg = tg + jnp.where(tile_start >= off, 1.0, 0.0)
    pos_ref[...] = pos.astype(I32)
    lane = lax.broadcasted_iota(I32, (8, LANES), 1)
    n_used = off * (1.0 / tm)
    tgi = jnp.minimum(tg, N_GROUPS - 1.0)
    tg_ref[...] = jnp.where(lane == LANES - 1, n_used, tgi).astype(I32)


def _positions(gidx2d, *, tm):
    n_rows = gidx2d.shape[0]
    return pl.pallas_call(
        functools.partial(_positions_body, n_rows=n_rows, tm=tm),
        out_shape=[jax.ShapeDtypeStruct((n_rows, LANES), I32), jax.ShapeDtypeStruct((8, LANES), I32)],
        name="moe_positions",
    )(gidx2d)


def _scatter_body(pos_ref, h_ref, init_ref, o_ref, sem, *, tm):
    del init_ref
    base = pl.program_id(0) * tm

    def row_copy(r):
        return pltpu.make_async_copy(h_ref.at[pl.ds(r, 1), :], o_ref.at[pl.ds(pos_ref[base + r], 1), :], sem)

    def start(r, c):
        row_copy(r).start()
        return c

    lax.fori_loop(0, tm, start, 0)

    def wait(r, c):
        row_copy(r).wait()
        return c

    lax.fori_loop(0, tm, wait, 0)


def _scatter_rows(pos, h, init, *, tm=256):
    t, d = h.shape
    return pl.pallas_call(
        functools.partial(_scatter_body, tm=tm),
        grid_spec=pltpu.PrefetchScalarGridSpec(
            num_scalar_prefetch=1,
            grid=(t // tm,),
            in_specs=[pl.BlockSpec((tm, d), lambda i, p: (i, 0)), pl.BlockSpec(memory_space=pl.ANY)],
            out_specs=pl.BlockSpec(memory_space=pl.ANY),
            scratch_shapes=[pltpu.SemaphoreType.DMA(())],
        ),
        out_shape=jax.ShapeDtypeStruct(init.shape, init.dtype),
        input_output_aliases={2: 0},
        compiler_params=_cparams(("arbitrary",)),
        name="moe_scatter",
    )(pos, h, init)


def _experts_body(tg_ref, x_ref, wr_ref, rb_ref, w1_ref, w3_ref, w2_ref, o_ref, xb_ref, gate_ref, *, tm):
    ti = pl.program_id(0)
    q = pl.program_id(1)
    grp = tg_ref[ti]
    used = ti < tg_ref[LANES - 1]
    lane = lax.broadcasted_iota(I32, (tm, LANES), 1)
    lane_f = lane.astype(F32)

    @pl.when(jnp.logical_not(used) & (q == 0))
    def _():
        o_ref[...] = jnp.zeros_like(o_ref)

    @pl.when(used & (q == 0))
    def _():
        xb = x_ref[...].astype(BF16)
        xb_ref[...] = xb
        logits = _router_logits(xb, wr_ref, rb_ref)
        is_g = (lane >= N_EXPERTS) & (lane < N_EXPERTS + N_GROUPS)
        gl = jnp.where(is_g, logits, -jnp.inf)
        gmax = jnp.max(gl, axis=-1, keepdims=True)
        gsum = jnp.sum(jnp.exp(gl - gmax), axis=-1, keepdims=True)
        gsel = jnp.sum(jnp.where(lane == N_EXPERTS + grp, logits, 0.0), axis=-1, keepdims=True)
        p_grp = jnp.exp(gsel - gmax) / gsum
        in_grp = (lane >= grp * EXPERTS_PER_GROUP) & (lane < (grp + 1) * EXPERTS_PER_GROUP)
        el = jnp.where(in_grp, logits, -jnp.inf)
        v1 = jnp.max(el, axis=-1, keepdims=True)
        i1 = jnp.min(jnp.where(el == v1, lane_f, float(LANES)), axis=-1, keepdims=True)
        el2 = jnp.where(lane_f == i1, -jnp.inf, el)
        v2 = jnp.max(el2, axis=-1, keepdims=True)
        i2 = jnp.min(jnp.where(el2 == v2, lane_f, float(LANES)), axis=-1, keepdims=True)
        e2 = jnp.exp(v2 - v1)
        w1 = p_grp / (1.0 + e2)
        w2 = p_grp * e2 / (1.0 + e2)
        gate_ref[...] = jnp.where(lane_f == i1, w1, 0.0) + jnp.where(lane_f == i2, w2, 0.0)
        o_ref[...] = jnp.zeros_like(o_ref)

    @pl.when(used)
    def _():
        xb = xb_ref[...]
        a = jnp.dot(xb, w1_ref[...], preferred_element_type=F32)
        u = jnp.dot(xb, w3_ref[...], preferred_element_type=F32)
        gates = gate_ref[...]
        per = MOE_SLAB // EXPERT_FF
        cols = []
        for j in range(per):
            e = grp * EXPERTS_PER_GROUP + q * per + j
            ge = jnp.sum(jnp.where(lane == e, gates, 0.0), axis=-1, keepdims=True)
            cols.append(jnp.broadcast_to(ge, (tm, EXPERT_FF)))
        hid = (_silu(a) * u * jnp.concatenate(cols, axis=1)).astype(BF16)
        o_ref[...] += jnp.dot(hid, w2_ref[...], preferred_element_type=F32)


def _experts(tile_group, xs, wr, rb, w1, w3, w2, *, tm):
    n_rows, d = xs.shape
    n_tiles = n_rows // tm
    return pl.pallas_call(
        functools.partial(_experts_body, tm=tm),
        grid_spec=pltpu.PrefetchScalarGridSpec(
            num_scalar_prefetch=1,
            grid=(n_tiles, MOE_SPLIT),
            in_specs=[
                pl.BlockSpec((tm, d), lambda i, q, tg: (i, 0)),
                pl.BlockSpec((d, LANES), lambda i, q, tg: (0, 0)),
                pl.BlockSpec((1, LANES), lambda i, q, tg: (0, 0)),
                pl.BlockSpec((d, MOE_SLAB), lambda i, q, tg: (0, tg[i] * MOE_SPLIT + q)),
                pl.BlockSpec((d, MOE_SLAB), lambda i, q, tg: (0, tg[i] * MOE_SPLIT + q)),
                pl.BlockSpec((MOE_SLAB, d), lambda i, q, tg: (tg[i] * MOE_SPLIT + q, 0)),
            ],
            out_specs=pl.BlockSpec((tm, d), lambda i, q, tg: (i, 0)),
            scratch_shapes=[pltpu.VMEM((tm, d), BF16), pltpu.VMEM((tm, LANES), F32)],
        ),
        out_shape=jax.ShapeDtypeStruct((n_rows, d), F32),
        compiler_params=_cparams(("arbitrary", "arbitrary")),
        name="moe_experts",
    )(tile_group, xs, wr, rb, w1, w3, w2)


def _combine_body(pos_ref, x_ref, mod_ref, ys_ref, o_ref, buf_ref, sem, *, tm, n_ctx):
    base = pl.program_id(0) * tm

    def row_copy(r):
        return pltpu.make_async_copy(ys_ref.at[pl.ds(pos_ref[base + r], 1), :], buf_ref.at[pl.ds(r, 1), :], sem)

    def start(r, c):
        row_copy(r).start()
        return c

    lax.fori_loop(0, tm, start, 0)

    def wait(r, c):
        row_copy(r).wait()
        return c

    lax.fori_loop(0, tm, wait, 0)
    row = base + lax.broadcasted_iota(I32, (tm, 1), 0)
    gate = _mod_rows(mod_ref, 5, row < n_ctx)
    o_ref[...] = x_ref[...] + gate * buf_ref[...]


def _combine(pos, x, mod, ys, *, n_ctx, tm=256):
    t, d = x.shape
    return pl.pallas_call(
        functools.partial(_combine_body, tm=tm, n_ctx=n_ctx),
        grid_spec=pltpu.PrefetchScalarGridSpec(
            num_scalar_prefetch=1,
            grid=(t // tm,),
            in_specs=[
                pl.BlockSpec((tm, d), lambda i, p: (i, 0)),
                pl.BlockSpec((16, 6 * d), lambda i, p: (0, 0)),
                pl.BlockSpec(memory_space=pl.ANY),
            ],
            out_specs=pl.BlockSpec((tm, d), lambda i, p: (i, 0)),
            scratch_shapes=[pltpu.VMEM((tm, d), F32), pltpu.SemaphoreType.DMA(())],
        ),
        out_shape=jax.ShapeDtypeStruct((t, d), F32),
        compiler_params=_cparams(("arbitrary",)),
        name="moe_combine",
    )(pos, x, mod, ys)


def _final_body(x_ref, g_ref, o_ref):
    x = x_ref[...]
    o_ref[...] = x * lax.rsqrt(jnp.mean(x * x, axis=-1, keepdims=True) + NORM_EPS) * g_ref[...]


def _final_norm(x, g, *, n_ctx, tm=256):
    t, d = x.shape
    off = n_ctx // tm
    return pl.pallas_call(
        _final_body,
        grid=((t - n_ctx) // tm,),
        in_specs=[pl.BlockSpec((tm, d), lambda i: (i + off, 0)), pl.BlockSpec((1, d), lambda i: (0, 0))],
        out_specs=pl.BlockSpec((tm, d), lambda i: (i, 0)),
        out_shape=jax.ShapeDtypeStruct((t - n_ctx, d), F32),
        compiler_params=_cparams(("parallel",)),
        name="final_norm",
    )(x, g.reshape(1, d))


def _rope_tables(n_ctx, seq):
    n_rows = seq // GRID_W
    row = jnp.repeat(jnp.arange(n_rows), GRID_W).astype(F32)
    col = jnp.tile(jnp.arange(GRID_W), n_rows).astype(F32)
    n_freq = ROPE_DIM // 4
    inv = ROPE_BASE ** (-jnp.arange(n_freq, dtype=F32) / n_freq)
    ang = jnp.concatenate([row[:, None] * inv, row[:, None] * inv, col[:, None] * inv, col[:, None] * inv], axis=-1)
    sign = jnp.where((jnp.arange(ROPE_DIM) % (ROPE_DIM // 2)) < ROPE_DIM // 4, -1.0, 1.0).astype(F32)
    cos = jnp.concatenate([jnp.ones((n_ctx, ROPE_DIM), F32), jnp.cos(ang)], axis=0)
    sin = jnp.concatenate([jnp.zeros((n_ctx, ROPE_DIM), F32), jnp.sin(ang) * sign], axis=0)
    return cos, sin


def _split_w_in(w):
    idx, acc = [], 0
    for s in IN_SPLITS[:-1]:
        acc += s
        idx.append(acc)
    return jnp.split(w, idx, axis=-1)


def _row_tile(t):
    for tm in (1056, 1024, 768, 512, 384, 256, 128):
        if t % tm == 0:
            return tm
    raise ValueError(f"token count {t} is not a multiple of {CHUNK}")


def kernel(x, c, ctx, c_ctx, norm1_g, norm2_g, mod_down, mod_up, mod_b, w_in, conv_w, conv_b, ssd_a_log, ssd_dt_bias, ssd_d, ssd_norm_g, ret_log_decay, attn_sink, w_branch_a, w_branch_b, w_branch_c, w_gate_up, w_out, router_group_w, router_group_b, router_expert_w, router_expert_b, moe_w1, moe_w3, moe_w2, final_norm_g):
    assert x.shape[0] == 1 and ctx.shape[0] == 1
    seq, d = x.shape[1], x.shape[2]
    n_ctx = ctx.shape[1]
    depth = w_in.shape[0]
    assert d == D_MODEL and seq % 256 == 0 and n_ctx % 256 == 0 and seq % GRID_W == 0
    t = n_ctx + seq
    tm = _row_tile(t)

    xres = jnp.concatenate([ctx[0], x[0]], axis=0)
    cc = jnp.zeros((16, d), F32).at[0].set(c_ctx).at[1].set(c[0])
    mods = _modulation(cc, mod_down, mod_up, mod_b)
    cos, sin = _rope_tables(n_ctx, seq)
    colscale = jnp.ones((1, N_ROPE_COLS), F32).at[:, COL_RK:COL_RK + RET_Q].set(RET_QK_DIM ** -0.5)
    e64 = (jnp.arange(LANES)[:, None] == (jnp.arange(SSD_D_INNER)[None, :] // SSD_HEAD_DIM)).astype(BF16)
    pad_lanes = lambda v: jnp.zeros((2, LANES), F32).at[:, :SSD_HEADS].set(v)

    for i in range(depth):
        mod = mods[i]
        z_w, xbc_w, dt_w, rq_w, rk_w, rv_w, rg_w, aq_w, ak_w, av_w, gd_w = _split_w_in(w_in[i])
        xs_w, b_w, c_w = jnp.split(xbc_w, [SSD_D_INNER, SSD_D_INNER + SSD_BC], axis=-1)
        w_proj = jnp.concatenate([aq_w, rq_w, rk_w, ak_w, av_w, b_w, c_w, z_w, xs_w, rv_w, rg_w, gd_w],
                                 axis=-1).astype(BF16)
        w_dt = jnp.zeros((d, DT_W), F32).at[:, :SSD_HEADS].set(dt_w[:, :SSD_HEADS]) \
            .at[:, LANES:LANES + SSD_HEADS].set(dt_w[:, SSD_HEADS:]).astype(BF16)
        cw = jnp.zeros((8, SSD_XBC), F32).at[:SSD_CONV].set(conv_w[i])
        cwx, cwb, cwc = jnp.split(cw, [SSD_D_INNER, SSD_D_INNER + SSD_BC], axis=-1)
        cbx, cbb, cbc = jnp.split(conv_b[i].reshape(1, SSD_XBC), [SSD_D_INNER, SSD_D_INNER + SSD_BC], axis=-1)

        h = _normmod(xres, norm1_g[i], mod, n_ctx=n_ctx, k_shift=0, k_scale=1, out_dtype=BF16)
        proj = _in_proj(h, w_proj, cos, sin, colscale, tm=tm)
        dtraw = _matmul(h, w_dt, tm=tm, tn=DT_W, out_dtype=F32, name="dt_proj")

        dtb = pad_lanes(ssd_dt_bias[i])
        alog = pad_lanes(ssd_a_log[i])
        yf, xs_c, b_c, c_c = _ssd_forward(proj, dtraw, cwx, cwb, cwc, cbx, cbb, cbc, dtb, alog, e64, n_ctx=n_ctx)
        d_x = jnp.repeat(ssd_d[i], SSD_HEAD_DIM).reshape(1, SSD_D_INNER)
        ya = _ssd_backward(proj, xs_c, b_c, c_c, dtraw, dtb, alog, e64, yf, d_x,
                           ssd_norm_g[i].reshape(1, SSD_D_INNER), n_ctx=n_ctx)

        rf = _retention(proj, ret_log_decay[i], None, backward=False, n_ctx=n_ctx)
        yb = _retention(proj, ret_log_decay[i], rf, backward=True, n_ctx=n_ctx)
        yc = _attention(proj, attn_sink[i], n_ctx=n_ctx)

        merged = _merge(ya, yb, yc, proj, w_branch_a[i].astype(BF16), w_branch_b[i].astype(BF16),
                        w_branch_c[i].astype(BF16), w_gate_up[i].astype(BF16), tm=768 if t % 768 == 0 else 256)
        xres = _outproj(merged, w_out[i].astype(BF16), xres, mod[:, 2 * d:3 * d], n_ctx=n_ctx, k_gate=0, tm=tm)

        wr = jnp.zeros((d, LANES), F32).at[:, :N_EXPERTS].set(router_expert_w[i]) \
            .at[:, N_EXPERTS:N_EXPERTS + N_GROUPS].set(router_group_w[i]).astype(BF16)
        rb = jnp.zeros((1, LANES), F32).at[0, :N_EXPERTS].set(router_expert_b[i]) \
            .at[0, N_EXPERTS:N_EXPERTS + N_GROUPS].set(router_group_b[i])
        h2, gidx = _norm2_router(xres, norm2_g[i], mod, wr, rb, n_ctx=n_ctx)
        n_idx_rows = -(-(t // LANES) // 8) * 8
        gidx2d = jnp.full((n_idx_rows * LANES,), N_GROUPS, I32).at[:t].set(gidx[:, 0]).reshape(n_idx_rows, LANES)
        pos2d, tile_group = _positions(gidx2d, tm=MOE_TM)
        pos = pos2d.reshape(-1)[:t]
        n_sorted = (t // MOE_TM + N_GROUPS) * MOE_TM
        xs_sorted = _scatter_rows(pos, h2, jnp.zeros((n_sorted, d), F32))
        w1 = jnp.transpose(moe_w1[i], (1, 0, 2)).reshape(d, N_EXPERTS * EXPERT_FF).astype(BF16)
        w3 = jnp.transpose(moe_w3[i], (1, 0, 2)).reshape(d, N_EXPERTS * EXPERT_FF).astype(BF16)
        w2 = moe_w2[i].reshape(N_EXPERTS * EXPERT_FF, d).astype(BF16)
        ys = _experts(tile_group[0], xs_sorted, wr, rb, w1, w3, w2, tm=MOE_TM)
        xres = _combine(pos, xres, mod, ys, n_ctx=n_ctx)

    out = _final_norm(xres, final_norm_g, n_ctx=n_ctx)
    return out[None]
```

```python
import functools
import math

import jax
import jax.numpy as jnp
from jax import lax
from jax.experimental import pallas as pl
from jax.experimental.pallas import tpu as pltpu

F32 = jnp.float32
BF16 = jnp.bfloat16
I32 = jnp.int32

D_MODEL = 4096
GRID_W = 64
CHUNK = 128
NORM_EPS = 1e-6
NEG_INF = -1e30
MOD_RANK = D_MODEL // 16
SSD_D_INNER = D_MODEL // 2
SSD_HEAD_DIM = 64
SSD_HEADS = SSD_D_INNER // SSD_HEAD_DIM
SSD_GROUPS = 4
SSD_HPG = SSD_HEADS // SSD_GROUPS
SSD_STATE = 128
SSD_CONV = 5
SSD_BC = SSD_GROUPS * SSD_STATE
RET_HEADS = 8
RET_QK_DIM = 128
RET_V_DIM = 2 * RET_QK_DIM
ATTN_HEADS = 16
ATTN_KV_HEADS = 4
ATTN_HEAD_DIM = 128
ATTN_GRP = ATTN_HEADS // ATTN_KV_HEADS
ROPE_DIM = 128
ROPE_BASE = 10000.0
GATE_RANK = D_MODEL // 16
N_GROUPS = 8
EXPERTS_PER_GROUP = 8
N_EXPERTS = N_GROUPS * EXPERTS_PER_GROUP
EXPERT_FF = D_MODEL // 32
SSD_XBC = SSD_D_INNER + 2 * SSD_BC
RET_Q = RET_HEADS * RET_QK_DIM
RET_V = RET_HEADS * RET_V_DIM
ATTN_Q = ATTN_HEADS * ATTN_HEAD_DIM
ATTN_KV = ATTN_KV_HEADS * ATTN_HEAD_DIM
IN_SPLITS = (SSD_D_INNER, SSD_XBC, 2 * SSD_HEADS, RET_Q, RET_Q, RET_V, RET_V, ATTN_Q, ATTN_KV, ATTN_KV, GATE_RANK)

LANES = 128
BF16_SUBLANES = 16
VMEM_LIMIT = 56 * 1024 * 1024

ROPE_AQ = 0
ROPE_RQ = ROPE_AQ + ATTN_Q
ROPE_RK = ROPE_RQ + RET_Q
ROPE_AK = ROPE_RK + RET_Q
ROPE_W = ROPE_AK + ATTN_KV
PL_Z = 0
PL_XS = PL_Z + SSD_D_INNER
PL_RV = PL_XS + SSD_D_INNER
PL_RG = PL_RV + RET_V
PL_AV = PL_RG + RET_V
PL_B = PL_AV + ATTN_KV
PL_C = PL_B + SSD_BC
PL_GD = PL_C + SSD_BC
PLAIN_W = PL_GD + GATE_RANK
PROJ_TN = 768
DT_W = 2 * LANES

MOE_TM = 256
MOE_SPLIT = 2
MOE_SLAB = EXPERTS_PER_GROUP * EXPERT_FF // MOE_SPLIT
ROW_TM = 256


def _cparams(sem):
    return pltpu.CompilerParams(dimension_semantics=sem, vmem_limit_bytes=VMEM_LIMIT)


def _silu(x):
    return x * (1.0 / (1.0 + jnp.exp(-x)))


def _split3(x):
    hi = x.astype(BF16)
    r1 = x - hi.astype(F32)
    mid = r1.astype(BF16)
    lo = (r1 - mid.astype(F32)).astype(BF16)
    return hi, mid, lo


def _dot01_left(m01, x):
    hi, mid, lo = _split3(x)
    d = lambda t: jnp.dot(m01, t, preferred_element_type=F32)
    return d(hi) + d(mid) + d(lo)


def _dot01_right(x, m01):
    hi, mid, lo = _split3(x)
    d = lambda t: jnp.dot(t, m01, preferred_element_type=F32)
    return d(hi) + d(mid) + d(lo)


def _mod_body(c_ref, down_ref, up_ref, b_ref, o_ref):
    s = _silu(c_ref[...])
    r = jnp.dot(s.astype(BF16), down_ref[...].astype(BF16), preferred_element_type=F32)
    o_ref[...] = jnp.dot(r.astype(BF16), up_ref[...].astype(BF16), preferred_element_type=F32) + b_ref[...]


def _modulation(cc, mod_down, mod_up, mod_b):
    depth = mod_down.shape[0]
    d = cc.shape[1]
    return pl.pallas_call(
        _mod_body,
        grid=(depth, 6),
        in_specs=[
            pl.BlockSpec((16, d), lambda l, j: (0, 0)),
            pl.BlockSpec((None, d, MOD_RANK), lambda l, j: (l, 0, 0)),
            pl.BlockSpec((None, MOD_RANK, d), lambda l, j: (l, 0, j)),
            pl.BlockSpec((None, 1, d), lambda l, j: (l, 0, j)),
        ],
        out_specs=pl.BlockSpec((None, 16, d), lambda l, j: (l, 0, j)),
        out_shape=jax.ShapeDtypeStruct((depth, 16, 6 * d), F32),
        compiler_params=_cparams(("arbitrary", "arbitrary")),
        name="modulation",
    )(cc, mod_down, mod_up, mod_b.reshape(depth, 1, 6 * d))


def _mod_row(mod_ref, k, first_row, n_ctx):
    d = D_MODEL
    r = jnp.where(first_row < n_ctx, 0, 1)
    return mod_ref[pl.ds(r, 1), k * d:(k + 1) * d]


def _norm_mod(x, g_ref, mod_ref, k_shift, k_scale, first_row, n_ctx):
    rstd = lax.rsqrt(jnp.mean(x * x, axis=-1, keepdims=True) + NORM_EPS)
    gain = g_ref[...] * (1.0 + _mod_row(mod_ref, k_scale, first_row, n_ctx))
    return x * rstd * gain + _mod_row(mod_ref, k_shift, first_row, n_ctx)


def _is_ctx_rows(first_row, tm, n_ctx):
    return (first_row + lax.broadcasted_iota(I32, (tm, 1), 0)) < n_ctx


def _layer_vec(layer, d):
    return pl.BlockSpec((None, 1, d), lambda *_: (layer, 0, 0))


def _layer_mod(layer, d):
    return pl.BlockSpec((None, 16, 6 * d), lambda *_: (layer, 0, 0))


def _normmod_body(x_ref, g_ref, mod_ref, o_ref, *, n_ctx, tm):
    o_ref[...] = _norm_mod(x_ref[...], g_ref, mod_ref, 0, 1, pl.program_id(0) * tm, n_ctx).astype(o_ref.dtype)


def _normmod(x, g, mods, layer, *, n_ctx, tm=ROW_TM):
    t, d = x.shape
    return pl.pallas_call(
        functools.partial(_normmod_body, n_ctx=n_ctx, tm=tm),
        grid=(t // tm,),
        in_specs=[pl.BlockSpec((tm, d), lambda i: (i, 0)), _layer_vec(layer, d), _layer_mod(layer, d)],
        out_specs=pl.BlockSpec((tm, d), lambda i: (i, 0)),
        out_shape=jax.ShapeDtypeStruct((t, d), BF16),
        compiler_params=_cparams(("parallel",)),
        name="normmod",
    )(x, g, mods)


def _proj_rope_body(x_ref, w_ref, cos_ref, sin_ref, cs_ref, o_ref, *, tm, tn):
    acc = jnp.dot(x_ref[...], w_ref[...], preferred_element_type=F32)
    cos = cos_ref[...]
    sin = sin_ref[...]
    lane = lax.broadcasted_iota(I32, (tm, LANES), 1)
    first_half = (lane % (ROPE_DIM // 2)) < (ROPE_DIM // 4)
    for h in range(tn // LANES):
        sl = slice(h * LANES, (h + 1) * LANES)
        a = acc[:, sl]
        rot = jnp.where(first_half, pltpu.roll(a, ROPE_DIM - ROPE_DIM // 4, 1), pltpu.roll(a, ROPE_DIM // 4, 1))
        o_ref[:, sl] = ((a * cos + rot * sin) * cs_ref[:, sl]).astype(o_ref.dtype)


def _mm_body(x_ref, w_ref, o_ref):
    o_ref[...] = jnp.dot(x_ref[...], w_ref[...], preferred_element_type=F32).astype(o_ref.dtype)


def _proj_rope(h, w, layer, cos, sin, colscale, *, tm):
    t, k = h.shape
    n = w.shape[2]
    tn = PROJ_TN
    return pl.pallas_call(
        functools.partial(_proj_rope_body, tm=tm, tn=tn),
        grid=(t // tm, n // tn),
        in_specs=[
            pl.BlockSpec((tm, k), lambda i, j: (i, 0)),
            pl.BlockSpec((None, k, tn), lambda i, j: (layer, 0, j)),
            pl.BlockSpec((tm, LANES), lambda i, j: (i, 0)),
            pl.BlockSpec((tm, LANES), lambda i, j: (i, 0)),
            pl.BlockSpec((1, tn), lambda i, j: (0, j)),
        ],
        out_specs=pl.BlockSpec((tm, tn), lambda i, j: (i, j)),
        out_shape=jax.ShapeDtypeStruct((t, n), BF16),
        compiler_params=_cparams(("parallel", "arbitrary")),
        name="proj_rope",
    )(h, w, cos, sin, colscale)


def _matmul(x, w, layer, *, tm, tn, out_dtype, name):
    t, k = x.shape
    n = w.shape[2]
    return pl.pallas_call(
        _mm_body,
        grid=(t // tm, n // tn),
        in_specs=[pl.BlockSpec((tm, k), lambda i, j: (i, 0)), pl.BlockSpec((None, k, tn), lambda i, j: (layer, 0, j))],
        out_specs=pl.BlockSpec((tm, tn), lambda i, j: (i, j)),
        out_shape=jax.ShapeDtypeStruct((t, n), out_dtype),
        compiler_params=_cparams(("parallel", "arbitrary")),
        name=name,
    )(x, w)


def _chunk_of_step(i, *, backward, n_ctx_chunks, n_chunks):
    if not backward:
        return i
    return jnp.where(i < n_ctx_chunks, n_ctx_chunks - 1 - i, n_chunks - 1 - (i - n_ctx_chunks))


def _tri(backward):
    l = lax.broadcasted_iota(I32, (CHUNK, CHUNK), 0)
    s = lax.broadcasted_iota(I32, (CHUNK, CHUNK), 1)
    return (s >= l) if backward else (s <= l)


def _conv_silu(ext_ref, prev_ref, cur_ref, next_ref, w_ref, b_ref, zero_prev, zero_next):
    half = SSD_CONV // 2
    pad = 8
    hp = prev_ref[...].astype(F32)[BF16_SUBLANES - pad:, :]
    hn = next_ref[...].astype(F32)[:pad, :]
    ext_ref[0:pad, :] = hp * jnp.where(zero_prev, 0.0, 1.0)
    ext_ref[pad:pad + CHUNK, :] = cur_ref[...].astype(F32)
    ext_ref[pad + CHUNK:, :] = hn * jnp.where(zero_next, 0.0, 1.0)
    acc = None
    for j in range(SSD_CONV):
        term = ext_ref[pad - half + j:pad - half + j + CHUNK, :] * w_ref[j:j + 1, :]
        acc = term if acc is None else acc + term
    return _silu(acc + b_ref[...])


def _ssd_body(*refs, backward, finalize, n_ctx_chunks, n_chunks):
    if finalize:
        (xs_ref, b_ref, c_ref, dt_ref, dtb_ref, alog_ref, e64_ref, yf_ref, z_ref, dx_ref, ng_ref,
         y_ref, state_ref) = refs
    else:
        (xs_p, xs_c, xs_n, b_p, b_c, b_n, c_p, c_c, c_n, wx_ref, wb_ref, wc_ref, bx_ref, bb_ref, bc_ref,
         dt_ref, dtb_ref, alog_ref, e64_ref,
         y_ref, xs_o, b_o, c_o, state_ref, extx_ref, extb_ref) = refs
    i = pl.program_id(0)
    chunk = _chunk_of_step(i, backward=backward, n_ctx_chunks=n_ctx_chunks, n_chunks=n_chunks)

    @pl.when(i == 0)
    def _():
        state_ref[...] = jnp.zeros_like(state_ref)

    if finalize:
        xs = xs_ref[...].astype(F32)
        bm = b_ref[...]
        cm = c_ref[...]
    else:
        zero_prev = (chunk == 0) | (chunk == n_ctx_chunks)
        zero_next = (chunk == n_ctx_chunks - 1) | (chunk == n_chunks - 1)
        xs = _conv_silu(extx_ref, xs_p, xs_c, xs_n, wx_ref, bx_ref, zero_prev, zero_next)
        bm = _conv_silu(extb_ref, b_p, b_c, b_n, wb_ref, bb_ref, zero_prev, zero_next).astype(BF16)
        cm = _conv_silu(extb_ref, c_p, c_c, c_n, wc_ref, bc_ref, zero_prev, zero_next).astype(BF16)
        xs_o[...] = xs.astype(BF16)
        b_o[...] = bm
        c_o[...] = cm
    xs_b = xs.astype(BF16)

    x = dt_ref[...] + dtb_ref[...]
    dtp = jnp.maximum(x, 0.0) + jnp.log(1.0 + jnp.exp(-jnp.abs(x)))
    la = -jnp.exp(alog_ref[...]) * dtp
    mask = _tri(backward)
    tri01 = jnp.where(mask, 1.0, 0.0).astype(BF16)
    acs = _dot01_left(tri01, la)
    total = acs[0:1, :] if backward else acs[CHUNK - 1:CHUNK, :]
    acs_t = acs.T
    dt_t = dtp.T
    wd_t = (jnp.exp(total - acs) * dtp).T
    cdec = jnp.broadcast_to(jnp.exp(total), (8, LANES))
    cdec_x = _dot01_right(cdec, e64_ref[...])[0:1, :]
    lane = lax.broadcasted_iota(I32, (CHUNK, LANES), 1)
    low = lane < SSD_HEAD_DIM

    y_pairs = []
    for g in range(SSD_GROUPS):
        q_g = cm[:, g * SSD_STATE:(g + 1) * SSD_STATE]
        k_g = bm[:, g * SSD_STATE:(g + 1) * SSD_STATE]
        scores = lax.dot_general(q_g, k_g, (((1,), (1,)), ((), ())), preferred_element_type=F32)
        k_t = k_g.astype(F32).T
        for pr in range(SSD_HPG // 2):
            ha = g * SSD_HPG + 2 * pr
            hb = ha + 1
            pair = ha // 2
            sl = slice(pair * LANES, (pair + 1) * LANES)
            col_a = jnp.broadcast_to(acs[:, ha:ha + 1], (CHUNK, CHUNK))
            col_b = jnp.broadcast_to(acs[:, hb:hb + 1], (CHUNK, CHUNK))
            dec_a = jnp.exp(jnp.where(mask, col_a - acs_t[ha:ha + 1, :], -jnp.inf))
            dec_b = jnp.exp(jnp.where(mask, col_b - acs_t[hb:hb + 1, :], -jnp.inf))
            m_a = scores * dec_a * dt_t[ha:ha + 1, :]
            m_b = scores * dec_b * dt_t[hb:hb + 1, :]
            s_a = k_t * wd_t[ha:ha + 1, :]
            s_b = k_t * wd_t[hb:hb + 1, :]
            lhs = jnp.concatenate(
                [jnp.concatenate([m_a, m_b], axis=1), jnp.concatenate([s_a, s_b], axis=1)], axis=0).astype(BF16)
            xp = xs_b[:, sl]
            zero = jnp.zeros_like(xp)
            rhs = jnp.concatenate([jnp.where(low, xp, zero), jnp.where(low, zero, xp)], axis=0)
            both = jnp.dot(lhs, rhs, preferred_element_type=F32)
            prev = state_ref[pair]
            inter = jnp.dot(q_g, prev.astype(BF16), preferred_element_type=F32)
            inter = inter * jnp.exp(jnp.where(low, col_a, col_b))
            y_pair = both[:CHUNK] + inter
            state_ref[pair] = prev * cdec_x[:, sl] + both[CHUNK:]
            if finalize:
                y_pairs.append(y_pair + yf_ref[:, sl] + dx_ref[:, sl] * xs[:, sl])
            else:
                y_ref[:, sl] = y_pair

    if finalize:
        y = jnp.concatenate(y_pairs, axis=1) * _silu(z_ref[...].astype(F32))
        gw = SSD_D_INNER // SSD_GROUPS
        for g in range(SSD_GROUPS):
            sl = slice(g * gw, (g + 1) * gw)
            yg = y[:, sl]
            yg = yg * lax.rsqrt(jnp.mean(yg * yg, axis=-1, keepdims=True) + NORM_EPS)
            y_ref[:, sl] = (yg * ng_ref[:, sl]).astype(y_ref.dtype)


def _dir_vec(layer, direction):
    return pl.BlockSpec((None, None, 1, LANES), lambda i: (layer, direction, 0, 0))


def _ssd_forward(proj, dtraw, conv_w, conv_b, dtb, alog, e64, layer, *, n_ctx):
    t = proj.shape[0]
    n_chunks = t // CHUNK
    ncc = n_ctx // CHUNK
    hb = CHUNK // BF16_SUBLANES
    n_hblk = t // BF16_SUBLANES

    def cur(col, w):
        return pl.BlockSpec((CHUNK, w), lambda i: (i, col // w))

    def prev(col, w):
        return pl.BlockSpec((BF16_SUBLANES, w), lambda i: (jnp.maximum(i * hb - 1, 0), col // w))

    def nxt(col, w):
        return pl.BlockSpec((BF16_SUBLANES, w), lambda i: (jnp.minimum((i + 1) * hb, n_hblk - 1), col // w))

    def cw(rows, col, w):
        return pl.BlockSpec((None, rows, w), lambda i: (layer, 0, col // w))

    in_specs = [
        prev(PL_XS, SSD_D_INNER), cur(PL_XS, SSD_D_INNER), nxt(PL_XS, SSD_D_INNER),
        prev(PL_B, SSD_BC), cur(PL_B, SSD_BC), nxt(PL_B, SSD_BC),
        prev(PL_C, SSD_BC), cur(PL_C, SSD_BC), nxt(PL_C, SSD_BC),
        cw(8, 0, SSD_D_INNER), cw(8, SSD_D_INNER, SSD_BC), cw(8, SSD_D_INNER + SSD_BC, SSD_BC),
        cw(1, 0, SSD_D_INNER), cw(1, SSD_D_INNER, SSD_BC), cw(1, SSD_D_INNER + SSD_BC, SSD_BC),
        pl.BlockSpec((CHUNK, LANES), lambda i: (i, 0)),
        _dir_vec(layer, 0), _dir_vec(layer, 0),
        pl.BlockSpec((LANES, SSD_D_INNER), lambda i: (0, 0)),
    ]
    row = lambda w: pl.BlockSpec((CHUNK, w), lambda i: (i, 0))
    out_shape = [
        jax.ShapeDtypeStruct((t, SSD_D_INNER), F32),
        jax.ShapeDtypeStruct((t, SSD_D_INNER), BF16),
        jax.ShapeDtypeStruct((t, SSD_BC), BF16),
        jax.ShapeDtypeStruct((t, SSD_BC), BF16),
    ]
    return pl.pallas_call(
        functools.partial(_ssd_body, backward=False, finalize=False, n_ctx_chunks=ncc, n_chunks=n_chunks),
        grid=(n_chunks,),
        in_specs=in_specs,
        out_specs=[row(SSD_D_INNER), row(SSD_D_INNER), row(SSD_BC), row(SSD_BC)],
        out_shape=out_shape,
        scratch_shapes=[
            pltpu.VMEM((SSD_HEADS // 2, SSD_STATE, LANES), F32),
            pltpu.VMEM((CHUNK + 16, SSD_D_INNER), F32),
            pltpu.VMEM((CHUNK + 16, SSD_BC), F32),
        ],
        compiler_params=_cparams(("arbitrary",)),
        name="ssd_fwd",
    )(proj, proj, proj, proj, proj, proj, proj, proj, proj,
      conv_w, conv_w, conv_w, conv_b, conv_b, conv_b, dtraw, dtb, alog, e64)


def _ssd_backward(proj, xs, bm, cm, dtraw, dtb, alog, e64, yf, d_x, norm_g, layer, *, n_ctx):
    t = proj.shape[0]
    n_chunks = t // CHUNK
    ncc = n_ctx // CHUNK
    cmap = functools.partial(_chunk_of_step, backward=True, n_ctx_chunks=ncc, n_chunks=n_chunks)
    row = lambda w, col=0: pl.BlockSpec((CHUNK, w), lambda i: (cmap(i), col // w))
    in_specs = [
        row(SSD_D_INNER), row(SSD_BC), row(SSD_BC),
        pl.BlockSpec((CHUNK, LANES), lambda i: (cmap(i), 1)),
        _dir_vec(layer, 1), _dir_vec(layer, 1),
        pl.BlockSpec((LANES, SSD_D_INNER), lambda i: (0, 0)),
        row(SSD_D_INNER), row(SSD_D_INNER, PL_Z), _layer_vec(layer, SSD_D_INNER), _layer_vec(layer, SSD_D_INNER),
    ]
    return pl.pallas_call(
        functools.partial(_ssd_body, backward=True, finalize=True, n_ctx_chunks=ncc, n_chunks=n_chunks),
        grid=(n_chunks,),
        in_specs=in_specs,
        out_specs=row(SSD_D_INNER),
        out_shape=jax.ShapeDtypeStruct((t, SSD_D_INNER), BF16),
        scratch_shapes=[pltpu.VMEM((SSD_HEADS // 2, SSD_STATE, LANES), F32)],
        compiler_params=_cparams(("arbitrary",)),
        name="ssd_bwd",
    )(xs, bm, cm, dtraw, dtb, alog, e64, yf, proj, d_x, norm_g)


def _ret_body(*refs, layer, backward, finalize):
    if finalize:
        dec_ref, q_ref, k_ref, v_ref, yf_ref, g_ref, y_ref, state_ref = refs
    else:
        dec_ref, q_ref, k_ref, v_ref, y_ref, state_ref = refs
    i = pl.program_id(0)

    @pl.when(i == 0)
    def _():
        state_ref[...] = jnp.zeros_like(state_ref)

    mask = _tri(backward)
    l_i = lax.broadcasted_iota(I32, (CHUNK, CHUNK), 0)
    s_i = lax.broadcasted_iota(I32, (CHUNK, CHUNK), 1)
    dist = ((s_i - l_i) if backward else (l_i - s_i)).astype(F32)
    pos = lax.broadcasted_iota(I32, (CHUNK, 1), 0).astype(F32)
    srow = lax.broadcasted_iota(I32, (1, CHUNK), 1).astype(F32)
    steps_in = (CHUNK - pos) if backward else (pos + 1.0)
    steps_out = srow if backward else (CHUNK - 1.0 - srow)
    d = 1 if backward else 0
    for h in range(RET_HEADS):
        a = dec_ref[layer, d, h]
        qh = q_ref[:, h * RET_QK_DIM:(h + 1) * RET_QK_DIM]
        kh = k_ref[:, h * RET_QK_DIM:(h + 1) * RET_QK_DIM]
        vh = v_ref[:, h * RET_V_DIM:(h + 1) * RET_V_DIM]
        scores = lax.dot_general(qh, kh, (((1,), (1,)), ((), ())), preferred_element_type=F32)
        m = scores * jnp.exp(jnp.where(mask, a * dist, -jnp.inf))
        intra = jnp.dot(m.astype(BF16), vh, preferred_element_type=F32)
        prev = state_ref[h]
        inter = jnp.dot(qh, prev.astype(BF16), preferred_element_type=F32) * jnp.exp(a * steps_in)
        kw = (kh.astype(F32).T * jnp.exp(a * steps_out)).astype(BF16)
        carry = jnp.exp(a * jnp.full((1, RET_V_DIM), float(CHUNK), F32))
        state_ref[h] = prev * carry + jnp.dot(kw, vh, preferred_element_type=F32)
        y = intra + inter
        sl = slice(h * RET_V_DIM, (h + 1) * RET_V_DIM)
        if finalize:
            y = y + yf_ref[:, sl]
            y = y - jnp.mean(y, axis=-1, keepdims=True)
            y = y * lax.rsqrt(jnp.mean(y * y, axis=-1, keepdims=True) + NORM_EPS)
            y_ref[:, sl] = (y * _silu(g_ref[:, sl].astype(F32))).astype(y_ref.dtype)
        else:
            y_ref[:, sl] = y


def _retention(proj_r, proj_p, log_decay, yf, layer, *, backward, n_ctx):
    t = proj_r.shape[0]
    n_chunks = t // CHUNK
    ncc = n_ctx // CHUNK
    cmap = functools.partial(_chunk_of_step, backward=backward, n_ctx_chunks=ncc, n_chunks=n_chunks)
    row = lambda w, col=0: pl.BlockSpec((CHUNK, w), lambda i: (cmap(i), col // w))
    in_specs = [pl.BlockSpec(memory_space=pltpu.SMEM), row(RET_Q, ROPE_RQ), row(RET_Q, ROPE_RK), row(RET_V, PL_RV)]
    args = [log_decay, proj_r, proj_r, proj_p]
    if backward:
        in_specs += [row(RET_V), row(RET_V, PL_RG)]
        args += [yf, proj_p]
    return pl.pallas_call(
        functools.partial(_ret_body, layer=layer, backward=backward, finalize=backward),
        grid=(n_chunks,),
        in_specs=in_specs,
        out_specs=row(RET_V),
        out_shape=jax.ShapeDtypeStruct((t, RET_V), BF16 if backward else F32),
        scratch_shapes=[pltpu.VMEM((RET_HEADS, RET_QK_DIM, RET_V_DIM), F32)],
        compiler_params=_cparams(("arbitrary",)),
        name="ret_bwd" if backward else "ret_fwd",
    )(*args)


def _attn_body(sink_ref, q_ref, kc_ref, vc_ref, kp_ref, kq_ref, kn_ref, vp_ref, vq_ref, vn_ref, o_ref,
               *, layer, n_ctx_chunks, n_chunks, n_ctx):
    i = pl.program_id(0)
    is_lat = i >= n_ctx_chunks
    rows = ATTN_GRP * CHUNK
    qi = lax.broadcasted_iota(I32, (rows, CHUNK), 0) % CHUNK
    kj = lax.broadcasted_iota(I32, (rows, CHUNK), 1)
    ok_prev = kj >= qi + jnp.where(i > n_ctx_chunks, 0, CHUNK)
    ok_cur = kj >= jnp.where(is_lat, 0, CHUNK)
    ok_next = kj <= qi - jnp.where(is_lat & (i < n_chunks - 1), 0, CHUNK)
    scale = ATTN_HEAD_DIM ** -0.5
    to_exp2 = scale * math.log2(math.e)
    for hk in range(ATTN_KV_HEADS):
        ksl = slice(hk * ATTN_HEAD_DIM, (hk + 1) * ATTN_HEAD_DIM)
        q = jnp.concatenate(
            [q_ref[:, (hk * ATTN_GRP + g) * ATTN_HEAD_DIM:(hk * ATTN_GRP + g + 1) * ATTN_HEAD_DIM]
             for g in range(ATTN_GRP)], axis=0)
        sink = jnp.concatenate(
            [jnp.full((CHUNK, 1), sink_ref[layer, hk * ATTN_GRP + g] * (1.0 / scale), F32)
             for g in range(ATTN_GRP)], axis=0)
        qk = lambda k: lax.dot_general(q, k, (((1,), (1,)), ((), ())), preferred_element_type=F32)
        s_c = qk(kc_ref[:, ksl])
        segs = [s_c[:, c * LANES:(c + 1) * LANES] for c in range(n_ctx // LANES)]
        segs.append(jnp.where(ok_prev, qk(kp_ref[:, ksl]), NEG_INF))
        segs.append(jnp.where(ok_cur, qk(kq_ref[:, ksl]), NEG_INF))
        segs.append(jnp.where(ok_next, qk(kn_ref[:, ksl]), NEG_INF))
        m_el = functools.reduce(jnp.maximum, segs)
        m = jnp.maximum(jnp.max(m_el, axis=-1, keepdims=True), sink)
        ps = [jnp.exp2((s - m) * to_exp2) for s in segs]
        den = jnp.exp2((sink - m) * to_exp2) + jnp.sum(functools.reduce(jnp.add, ps), axis=-1, keepdims=True)
        nc = n_ctx // LANES
        p_c = jnp.concatenate(ps[:nc], axis=1).astype(BF16)
        pv = lambda p, v: jnp.dot(p, v, preferred_element_type=F32)
        o = (pv(p_c, vc_ref[:, ksl]) + pv(ps[nc].astype(BF16), vp_ref[:, ksl])
             + pv(ps[nc + 1].astype(BF16), vq_ref[:, ksl]) + pv(ps[nc + 2].astype(BF16), vn_ref[:, ksl])) / den
        for g in range(ATTN_GRP):
            h = hk * ATTN_GRP + g
            o_ref[:, h * ATTN_HEAD_DIM:(h + 1) * ATTN_HEAD_DIM] = o[g * CHUNK:(g + 1) * CHUNK].astype(o_ref.dtype)


def _attention(proj_r, proj_p, sink, layer, *, n_ctx):
    t = proj_r.shape[0]
    n_chunks = t // CHUNK
    ncc = n_ctx // CHUNK
    kcol = ROPE_AK // ATTN_KV
    vcol = PL_AV // ATTN_KV
    lat = lambda i, off: jnp.clip(i + off, ncc, n_chunks - 1)
    in_specs = [
        pl.BlockSpec(memory_space=pltpu.SMEM),
        pl.BlockSpec((CHUNK, ATTN_Q), lambda i: (i, ROPE_AQ // ATTN_Q)),
        pl.BlockSpec((n_ctx, ATTN_KV), lambda i: (0, kcol)),
        pl.BlockSpec((n_ctx, ATTN_KV), lambda i: (0, vcol)),
        pl.BlockSpec((CHUNK, ATTN_KV), lambda i: (lat(i, -1), kcol)),
        pl.BlockSpec((CHUNK, ATTN_KV), lambda i: (i, kcol)),
        pl.BlockSpec((CHUNK, ATTN_KV), lambda i: (lat(i, 1), kcol)),
        pl.BlockSpec((CHUNK, ATTN_KV), lambda i: (lat(i, -1), vcol)),
        pl.BlockSpec((CHUNK, ATTN_KV), lambda i: (i, vcol)),
        pl.BlockSpec((CHUNK, ATTN_KV), lambda i: (lat(i, 1), vcol)),
    ]
    return pl.pallas_call(
        functools.partial(_attn_body, layer=layer, n_ctx_chunks=ncc, n_chunks=n_chunks, n_ctx=n_ctx),
        grid=(n_chunks,),
        in_specs=in_specs,
        out_specs=pl.BlockSpec((CHUNK, ATTN_Q), lambda i: (i, 0)),
        out_shape=jax.ShapeDtypeStruct((t, ATTN_Q), BF16),
        compiler_params=_cparams(("parallel",)),
        name="attention",
    )(sink, proj_r, proj_r, proj_p, proj_r, proj_r, proj_r, proj_p, proj_p, proj_p)


def _merge_body(ya_ref, yb_ref, yc_ref, gd_ref, wa_ref, wb_ref, wc_ref, ga_ref, gb_ref, gc_ref, o_ref):
    gd = gd_ref[...]

    def branch(y_ref, w_ref, g_ref):
        gate = jax.nn.sigmoid(jnp.dot(gd, g_ref[...], preferred_element_type=F32))
        return gate * jnp.dot(y_ref[...], w_ref[...], preferred_element_type=F32)

    o_ref[...] = (branch(ya_ref, wa_ref, ga_ref) + branch(yb_ref, wb_ref, gb_ref)
                  + branch(yc_ref, wc_ref, gc_ref)).astype(o_ref.dtype)


def _merge(ya, yb, yc, proj_p, wa, wb, wc, wg, layer, *, tm, tn=512):
    t = ya.shape[0]
    d = wa.shape[2]
    nj = d // tn
    act = lambda w: pl.BlockSpec((tm, w), lambda i, j: (i, 0))
    wsp = lambda k: pl.BlockSpec((None, k, tn), lambda i, j: (layer, 0, j))
    gsp = lambda b: pl.BlockSpec((None, GATE_RANK, tn), lambda i, j: (layer, 0, b * nj + j))
    return pl.pallas_call(
        _merge_body,
        grid=(t // tm, nj),
        in_specs=[act(SSD_D_INNER), act(RET_V), act(ATTN_Q),
                  pl.BlockSpec((tm, GATE_RANK), lambda i, j: (i, PL_GD // GATE_RANK)),
                  wsp(SSD_D_INNER), wsp(RET_V), wsp(ATTN_Q), gsp(0), gsp(1), gsp(2)],
        out_specs=pl.BlockSpec((tm, tn), lambda i, j: (i, j)),
        out_shape=jax.ShapeDtypeStruct((t, d), BF16),
        compiler_params=_cparams(("parallel", "arbitrary")),
        name="merge",
    )(ya, yb, yc, proj_p, wa, wb, wc, wg, wg, wg)


def _outproj_body(m_ref, w_ref, x_ref, mod_ref, o_ref, *, n_ctx, tm):
    acc = jnp.dot(m_ref[...], w_ref[...], preferred_element_type=F32)
    is_ctx = _is_ctx_rows(pl.program_id(0) * tm, tm, n_ctx)
    gate = jnp.where(is_ctx, mod_ref[0:1, :], mod_ref[1:2, :])
    o_ref[...] = x_ref[...] + gate * acc


def _outproj(merged, w, x, mods, layer, *, n_ctx, tm, tn=512):
    t, k = merged.shape
    d = w.shape[2]
    nj = d // tn
    k_gate = 2
    return pl.pallas_call(
        functools.partial(_outproj_body, n_ctx=n_ctx, tm=tm),
        grid=(t // tm, nj),
        in_specs=[
            pl.BlockSpec((tm, k), lambda i, j: (i, 0)),
            pl.BlockSpec((None, k, tn), lambda i, j: (layer, 0, j)),
            pl.BlockSpec((tm, tn), lambda i, j: (i, j)),
            pl.BlockSpec((None, 16, tn), lambda i, j: (layer, 0, k_gate * nj + j)),
        ],
        out_specs=pl.BlockSpec((tm, tn), lambda i, j: (i, j)),
        out_shape=jax.ShapeDtypeStruct((t, d), F32),
        compiler_params=_cparams(("parallel", "arbitrary")),
        name="outproj",
    )(merged, w, x, mods)


def _router_logits(hb, wr_ref, rb_ref):
    return jnp.dot(hb, wr_ref[...], preferred_element_type=F32) + rb_ref[...]


def _router_body(x_ref, g_ref, mod_ref, wr_ref, rb_ref, gi_ref, *, n_ctx, tm):
    h = _norm_mod(x_ref[...], g_ref, mod_ref, 3, 4, pl.program_id(0) * tm, n_ctx)
    logits = _router_logits(h.astype(BF16), wr_ref, rb_ref)
    lane = lax.broadcasted_iota(I32, (tm, LANES), 1)
    is_g = (lane >= N_EXPERTS) & (lane < N_EXPERTS + N_GROUPS)
    gl = jnp.where(is_g, logits, -jnp.inf)
    gmax = jnp.max(gl, axis=-1, keepdims=True)
    first = jnp.min(jnp.where(gl == gmax, (lane - N_EXPERTS).astype(F32), float(N_GROUPS)), axis=-1, keepdims=True)
    gi_ref[...] = first.astype(I32)


def _router(x, g, mods, wr, rb, layer, *, n_ctx, tm=ROW_TM):
    t, d = x.shape
    return pl.pallas_call(
        functools.partial(_router_body, n_ctx=n_ctx, tm=tm),
        grid=(t // tm,),
        in_specs=[
            pl.BlockSpec((tm, d), lambda i: (i, 0)),
            _layer_vec(layer, d), _layer_mod(layer, d),
            pl.BlockSpec((None, d, LANES), lambda i: (layer, 0, 0)),
            _layer_vec(layer, LANES),
        ],
        out_specs=pl.BlockSpec((tm, 1), lambda i: (i, 0)),
        out_shape=jax.ShapeDtypeStruct((t, 1), I32),
        compiler_params=_cparams(("parallel",)),
        name="moe_router",
    )(x, g, mods, wr, rb)


def _positions_body(g_ref, pos_ref, tg_ref, *, n_rows, tm):
    gidx = g_ref[...]
    li = lax.broadcasted_iota(I32, (LANES, LANES), 0)
    lj = lax.broadcasted_iota(I32, (LANES, LANES), 1)
    upper = jnp.where(li <= lj, 1.0, 0.0).astype(BF16)
    ri = lax.broadcasted_iota(I32, (n_rows, n_rows), 0)
    rj = lax.broadcasted_iota(I32, (n_rows, n_rows), 1)
    strict = jnp.where(rj < ri, 1.0, 0.0).astype(BF16)
    tile_start = (lax.broadcasted_iota(I32, (8, LANES), 1) * tm).astype(F32)
    pos = jnp.zeros((n_rows, LANES), F32)
    tg = jnp.zeros((8, LANES), F32)
    off = jnp.zeros((1, 1), F32)
    for g in range(N_GROUPS):
        mk = jnp.where(gidx == g, 1.0, 0.0)
        within = jnp.dot(mk.astype(BF16), upper, preferred_element_type=F32)
        rowtot = jnp.broadcast_to(within[:, LANES - 1:LANES], (n_rows, LANES))
        before = jnp.dot(strict, rowtot.astype(BF16), preferred_element_type=F32)
        pos = pos + mk * (off + before + within - 1.0)
        count = jnp.sum(jnp.sum(mk, axis=-1, keepdims=True), axis=0, keepdims=True)
        padded = jnp.floor((count + (tm - 1.0)) * (1.0 / tm)) * tm
        off = off + padded
        tg = tg + jnp.where(tile_start >= off, 1.0, 0.0)
    pos_ref[...] = pos.astype(I32)
    lane = lax.broadcasted_iota(I32, (8, LANES), 1)
    n_used = off * (1.0 / tm)
    tgi = jnp.minimum(tg, N_GROUPS - 1.0)
    tg_ref[...] = jnp.where(lane == LANES - 1, n_used, tgi).astype(I32)


def _positions(gidx2d, *, tm):
    n_rows = gidx2d.shape[0]
    return pl.pallas_call(
        functools.partial(_positions_body, n_rows=n_rows, tm=tm),
        out_shape=[jax.ShapeDtypeStruct((n_rows, LANES), I32), jax.ShapeDtypeStruct((8, LANES), I32)],
        name="moe_positions",
    )(gidx2d)


def _scatter_body(pos_ref, x_ref, g_ref, mod_ref, init_ref, o_ref, h_ref, sem, *, tm, n_ctx):
    del init_ref
    base = pl.program_id(0) * tm
    h_ref[...] = _norm_mod(x_ref[...], g_ref, mod_ref, 3, 4, base, n_ctx)

    def row_copy(r):
        return pltpu.make_async_copy(h_ref.at[pl.ds(r, 1), :], o_ref.at[pl.ds(pos_ref[base + r], 1), :], sem)

    def start(r, c):
        row_copy(r).start()
        return c

    lax.fori_loop(0, tm, start, 0)

    def wait(r, c):
        row_copy(r).wait()
        return c

    lax.fori_loop(0, tm, wait, 0)


def _scatter_rows(pos, x, g, mods, init, layer, *, n_ctx, tm=ROW_TM):
    t, d = x.shape
    return pl.pallas_call(
        functools.partial(_scatter_body, tm=tm, n_ctx=n_ctx),
        grid_spec=pltpu.PrefetchScalarGridSpec(
            num_scalar_prefetch=1,
            grid=(t // tm,),
            in_specs=[pl.BlockSpec((tm, d), lambda i, p: (i, 0)), _layer_vec(layer, d), _layer_mod(layer, d),
                      pl.BlockSpec(memory_space=pl.ANY)],
            out_specs=pl.BlockSpec(memory_space=pl.ANY),
            scratch_shapes=[pltpu.VMEM((tm, d), F32), pltpu.SemaphoreType.DMA(())],
        ),
        out_shape=jax.ShapeDtypeStruct(init.shape, init.dtype),
        input_output_aliases={4: 0},
        compiler_params=_cparams(("arbitrary",)),
        name="moe_scatter",
    )(pos, x, g, mods, init)


def _experts_body(tg_ref, x_ref, wr_ref, rb_ref, w1_ref, w3_ref, w2_ref, o_ref, xb_ref, gate_ref, *, tm):
    ti = pl.program_id(0)
    q = pl.program_id(1)
    grp = tg_ref[ti]
    used = ti < tg_ref[LANES - 1]
    lane = lax.broadcasted_iota(I32, (tm, LANES), 1)
    lane_f = lane.astype(F32)

    @pl.when(jnp.logical_not(used) & (q == 0))
    def _():
        o_ref[...] = jnp.zeros_like(o_ref)

    @pl.when(used & (q == 0))
    def _():
        xb = x_ref[...].astype(BF16)
        xb_ref[...] = xb
        logits = _router_logits(xb, wr_ref, rb_ref)
        is_g = (lane >= N_EXPERTS) & (lane < N_EXPERTS + N_GROUPS)
        gl = jnp.where(is_g, logits, -jnp.inf)
        gmax = jnp.max(gl, axis=-1, keepdims=True)
        gsum = jnp.sum(jnp.exp(gl - gmax), axis=-1, keepdims=True)
        gsel = jnp.sum(jnp.where(lane == N_EXPERTS + grp, logits, 0.0), axis=-1, keepdims=True)
        p_grp = jnp.exp(gsel - gmax) / gsum
        in_grp = (lane >= grp * EXPERTS_PER_GROUP) & (lane < (grp + 1) * EXPERTS_PER_GROUP)
        el = jnp.where(in_grp, logits, -jnp.inf)
        v1 = jnp.max(el, axis=-1, keepdims=True)
        i1 = jnp.min(jnp.where(el == v1, lane_f, float(LANES)), axis=-1, keepdims=True)
        el2 = jnp.where(lane_f == i1, -jnp.inf, el)
        v2 = jnp.max(el2, axis=-1, keepdims=True)
        i2 = jnp.min(jnp.where(el2 == v2, lane_f, float(LANES)), axis=-1, keepdims=True)
        e2 = jnp.exp(v2 - v1)
        w1 = p_grp / (1.0 + e2)
        w2 = p_grp * e2 / (1.0 + e2)
        gate_ref[...] = jnp.where(lane_f == i1, w1, 0.0) + jnp.where(lane_f == i2, w2, 0.0)

    @pl.when(used)
    def _():
        xb = xb_ref[...]
        a = jnp.dot(xb, w1_ref[...], preferred_element_type=F32)
        u = jnp.dot(xb, w3_ref[...], preferred_element_type=F32)
        gates = gate_ref[...]
        per = MOE_SLAB // EXPERT_FF
        cols = []
        for j in range(per):
            e = grp * EXPERTS_PER_GROUP + q * per + j
            ge = jnp.sum(jnp.where(lane == e, gates, 0.0), axis=-1, keepdims=True)
            cols.append(jnp.broadcast_to(ge, (tm, EXPERT_FF)))
        hid = (_silu(a) * u * jnp.concatenate(cols, axis=1)).astype(BF16)
        part = jnp.dot(hid, w2_ref[...], preferred_element_type=F32)

        @pl.when(q == 0)
        def _():
            o_ref[...] = part

        @pl.when(q > 0)
        def _():
            o_ref[...] += part


def _experts(tile_group, xs, wr, rb, w1, w3, w2, layer, *, tm):
    n_rows, d = xs.shape
    n_tiles = n_rows // tm
    return pl.pallas_call(
        functools.partial(_experts_body, tm=tm),
        grid_spec=pltpu.PrefetchScalarGridSpec(
            num_scalar_prefetch=1,
            grid=(n_tiles, MOE_SPLIT),
            in_specs=[
                pl.BlockSpec((tm, d), lambda i, q, tg: (i, 0)),
                pl.BlockSpec((None, d, LANES), lambda i, q, tg: (layer, 0, 0)),
                pl.BlockSpec((None, 1, LANES), lambda i, q, tg: (layer, 0, 0)),
                pl.BlockSpec((None, d, MOE_SLAB), lambda i, q, tg: (layer, 0, tg[i] * MOE_SPLIT + q)),
                pl.BlockSpec((None, d, MOE_SLAB), lambda i, q, tg: (layer, 0, tg[i] * MOE_SPLIT + q)),
                pl.BlockSpec((None, MOE_SLAB, d), lambda i, q, tg: (layer, tg[i] * MOE_SPLIT + q, 0)),
            ],
            out_specs=pl.BlockSpec((tm, d), lambda i, q, tg: (i, 0)),
            scratch_shapes=[pltpu.VMEM((tm, d), BF16), pltpu.VMEM((tm, LANES), F32)],
        ),
        out_shape=jax.ShapeDtypeStruct((n_rows, d), F32),
        compiler_params=_cparams(("arbitrary", "arbitrary")),
        name="moe_experts",
    )(tile_group, xs, wr, rb, w1, w3, w2)


def _combine_body(pos_ref, x_ref, mod_ref, ys_ref, g_ref, modn_ref, *rest, tm, n_ctx, first_row, final):
    if final:
        o_ref, buf_ref, sem = rest
    else:
        o_ref, h_ref, buf_ref, sem = rest
    base = first_row + pl.program_id(0) * tm

    def row_copy(r):
        return pltpu.make_async_copy(ys_ref.at[pl.ds(pos_ref[base + r], 1), :], buf_ref.at[pl.ds(r, 1), :], sem)

    def start(r, c):
        row_copy(r).start()
        return c

    lax.fori_loop(0, tm, start, 0)

    def wait(r, c):
        row_copy(r).wait()
        return c

    lax.fori_loop(0, tm, wait, 0)
    x = x_ref[...] + _mod_row(mod_ref, 5, base, n_ctx) * buf_ref[...]
    if final:
        o_ref[...] = x * lax.rsqrt(jnp.mean(x * x, axis=-1, keepdims=True) + NORM_EPS) * g_ref[...]
    else:
        o_ref[...] = x
        h_ref[...] = _norm_mod(x, g_ref, modn_ref, 0, 1, base, n_ctx).astype(h_ref.dtype)


def _combine(pos, x, mods, ys, g_next, layer, *, n_ctx, final, tm=ROW_TM):
    t, d = x.shape
    first_tile = n_ctx // tm if final else 0
    n_out = t - first_tile * tm
    next_layer = layer if final else layer + 1
    g_spec = pl.BlockSpec((1, d), lambda i, p: (0, 0)) if final else _layer_vec(next_layer, d)
    out_specs = [pl.BlockSpec((tm, d), lambda i, p: (i, 0))]
    out_shape = [jax.ShapeDtypeStruct((n_out, d), F32)]
    if not final:
        out_specs.append(pl.BlockSpec((tm, d), lambda i, p: (i, 0)))
        out_shape.append(jax.ShapeDtypeStruct((t, d), BF16))
    return pl.pallas_call(
        functools.partial(_combine_body, tm=tm, n_ctx=n_ctx, first_row=first_tile * tm, final=final),
        grid_spec=pltpu.PrefetchScalarGridSpec(
            num_scalar_prefetch=1,
            grid=(n_out // tm,),
            in_specs=[
                pl.BlockSpec((tm, d), lambda i, p: (i + first_tile, 0)),
                _layer_mod(layer, d),
                pl.BlockSpec(memory_space=pl.ANY),
                g_spec,
                _layer_mod(next_layer, d),
            ],
            out_specs=out_specs,
            scratch_shapes=[pltpu.VMEM((tm, d), F32), pltpu.SemaphoreType.DMA(())],
        ),
        out_shape=out_shape,
        compiler_params=_cparams(("arbitrary",)),
        name="moe_combine_final" if final else "moe_combine",
    )(pos, x, mods, ys, g_next, mods)


def _rope_tables(n_ctx, seq):
    n_rows = seq // GRID_W
    row = jnp.repeat(jnp.arange(n_rows), GRID_W).astype(F32)
    col = jnp.tile(jnp.arange(GRID_W), n_rows).astype(F32)
    n_freq = ROPE_DIM // 4
    inv = ROPE_BASE ** (-jnp.arange(n_freq, dtype=F32) / n_freq)
    ang = jnp.concatenate([row[:, None] * inv, row[:, None] * inv, col[:, None] * inv, col[:, None] * inv], axis=-1)
    sign = jnp.where((jnp.arange(ROPE_DIM) % (ROPE_DIM // 2)) < ROPE_DIM // 4, -1.0, 1.0).astype(F32)
    cos = jnp.concatenate([jnp.ones((n_ctx, ROPE_DIM), F32), jnp.cos(ang)], axis=0)
    sin = jnp.concatenate([jnp.zeros((n_ctx, ROPE_DIM), F32), jnp.sin(ang) * sign], axis=0)
    return cos, sin


def _split_w_in(w):
    idx, acc = [], 0
    for s in IN_SPLITS[:-1]:
        acc += s
        idx.append(acc)
    return jnp.split(w, idx, axis=-1)


def _row_tile(t):
    for tm in (1056, 1024, 768, 512, 384, 256):
        if t % tm == 0:
            return tm
    raise ValueError(f"token count {t} is not a multiple of 256")


def kernel(x, c, ctx, c_ctx, norm1_g, norm2_g, mod_down, mod_up, mod_b, w_in, conv_w, conv_b, ssd_a_log, ssd_dt_bias, ssd_d, ssd_norm_g, ret_log_decay, attn_sink, w_branch_a, w_branch_b, w_branch_c, w_gate_up, w_out, router_group_w, router_group_b, router_expert_w, router_expert_b, moe_w1, moe_w3, moe_w2, final_norm_g):
    assert x.shape[0] == 1 and ctx.shape[0] == 1
    seq, d = x.shape[1], x.shape[2]
    n_ctx = ctx.shape[1]
    depth = w_in.shape[0]
    assert d == D_MODEL and seq % ROW_TM == 0 and n_ctx % ROW_TM == 0 and seq % GRID_W == 0
    t = n_ctx + seq
    tm = _row_tile(t)
    tm_merge = 768 if t % 768 == 0 else ROW_TM

    z_w, xbc_w, dt_w, rq_w, rk_w, rv_w, rg_w, aq_w, ak_w, av_w, gd_w = _split_w_in(w_in)
    xs_w, b_w, c_w = jnp.split(xbc_w, [SSD_D_INNER, SSD_D_INNER + SSD_BC], axis=-1)
    w_rope = jnp.concatenate([aq_w, rq_w, rk_w, ak_w], axis=-1).astype(BF16)
    w_plain = jnp.concatenate([z_w, xs_w, rv_w, rg_w, av_w, b_w, c_w, gd_w], axis=-1).astype(BF16)
    zpad = jnp.zeros((depth, d, LANES - SSD_HEADS), F32)
    w_dt = jnp.concatenate([dt_w[..., :SSD_HEADS], zpad, dt_w[..., SSD_HEADS:], zpad], axis=-1).astype(BF16)
    wa = w_branch_a.astype(BF16)
    wb = w_branch_b.astype(BF16)
    wc = w_branch_c.astype(BF16)
    wg = w_gate_up.astype(BF16)
    wo = w_out.astype(BF16)
    wr = jnp.concatenate([router_expert_w, router_group_w,
                          jnp.zeros((depth, d, LANES - N_EXPERTS - N_GROUPS), F32)], axis=-1).astype(BF16)
    rb = jnp.concatenate([router_expert_b, router_group_b,
                          jnp.zeros((depth, LANES - N_EXPERTS - N_GROUPS), F32)], axis=-1).reshape(depth, 1, LANES)
    w1 = jnp.transpose(moe_w1, (0, 2, 1, 3)).reshape(depth, d, N_EXPERTS * EXPERT_FF).astype(BF16)
    w3 = jnp.transpose(moe_w3, (0, 2, 1, 3)).reshape(depth, d, N_EXPERTS * EXPERT_FF).astype(BF16)
    w2 = moe_w2.reshape(depth, N_EXPERTS * EXPERT_FF, d).astype(BF16)
    conv_w8 = jnp.concatenate([conv_w, jnp.zeros((depth, 8 - SSD_CONV, SSD_XBC), F32)], axis=1)
    conv_b3 = conv_b.reshape(depth, 1, SSD_XBC)
    hpad = jnp.zeros((depth, 2, LANES - SSD_HEADS), F32)
    dtb = jnp.concatenate([ssd_dt_bias, hpad], axis=-1).reshape(depth, 2, 1, LANES)
    alog = jnp.concatenate([ssd_a_log, hpad], axis=-1).reshape(depth, 2, 1, LANES)
    d_x = jnp.repeat(ssd_d, SSD_HEAD_DIM, axis=-1).reshape(depth, 1, SSD_D_INNER)
    ssd_g = ssd_norm_g.reshape(depth, 1, SSD_D_INNER)
    g1 = norm1_g.reshape(depth, 1, d)
    g2 = norm2_g.reshape(depth, 1, d)

    xres = jnp.concatenate([ctx[0], x[0]], axis=0)
    cc = jnp.zeros((16, d), F32).at[0].set(c_ctx).at[1].set(c[0])
    mods = _modulation(cc, mod_down, mod_up, mod_b)
    cos, sin = _rope_tables(n_ctx, seq)
    colscale = jnp.ones((1, ROPE_W), F32).at[:, ROPE_RK:ROPE_RK + RET_Q].set(RET_QK_DIM ** -0.5)
    e64 = (jnp.arange(LANES)[:, None] == (jnp.arange(SSD_D_INNER)[None, :] // SSD_HEAD_DIM)).astype(BF16)
    n_idx_rows = -(-(t // LANES) // 8) * 8
    n_sorted = (t // MOE_TM + N_GROUPS) * MOE_TM
    sorted_buf = jnp.zeros((n_sorted, d), F32)

    h = _normmod(xres, g1, mods, 0, n_ctx=n_ctx)
    out = None
    for i in range(depth):
        proj_r = _proj_rope(h, w_rope, i, cos, sin, colscale, tm=tm)
        proj_p = _matmul(h, w_plain, i, tm=tm, tn=PROJ_TN, out_dtype=BF16, name="proj_plain")
        dtraw = _matmul(h, w_dt, i, tm=tm, tn=DT_W, out_dtype=F32, name="proj_dt")

        yf, xs_c, b_c, c_c = _ssd_forward(proj_p, dtraw, conv_w8, conv_b3, dtb, alog, e64, i, n_ctx=n_ctx)
        ya = _ssd_backward(proj_p, xs_c, b_c, c_c, dtraw, dtb, alog, e64, yf, d_x, ssd_g, i, n_ctx=n_ctx)
        rf = _retention(proj_r, proj_p, ret_log_decay, None, i, backward=False, n_ctx=n_ctx)
        yb = _retention(proj_r, proj_p, ret_log_decay, rf, i, backward=True, n_ctx=n_ctx)
        yc = _attention(proj_r, proj_p, attn_sink, i, n_ctx=n_ctx)

        merged = _merge(ya, yb, yc, proj_p, wa, wb, wc, wg, i, tm=tm_merge)
        xres = _outproj(merged, wo, xres, mods, i, n_ctx=n_ctx, tm=tm)

        gidx = _router(xres, g2, mods, wr, rb, i, n_ctx=n_ctx)
        gidx2d = jnp.concatenate([gidx[:, 0], jnp.full((n_idx_rows * LANES - t,), N_GROUPS, I32)]).reshape(n_idx_rows, LANES)
        pos2d, tile_group = _positions(gidx2d, tm=MOE_TM)
        pos = pos2d.reshape(-1)[:t]
        sorted_buf = _scatter_rows(pos, xres, g2, mods, sorted_buf, i, n_ctx=n_ctx)
        ys = _experts(tile_group[0], sorted_buf, wr, rb, w1, w3, w2, i, tm=MOE_TM)
        if i + 1 < depth:
            xres, h = _combine(pos, xres, mods, ys, g1, i, n_ctx=n_ctx, final=False)
        else:
            (out,) = _combine(pos, xres, mods, ys, final_norm_g.reshape(1, d), i, n_ctx=n_ctx, final=True)
    return out[None]
```

```python
import functools
import math

import jax
import jax.numpy as jnp
from jax import lax
from jax.experimental import pallas as pl
from jax.experimental.pallas import tpu as pltpu

F32 = jnp.float32
BF16 = jnp.bfloat16
I32 = jnp.int32

D_MODEL = 4096
GRID_W = 64
CHUNK = 128
NORM_EPS = 1e-6
NEG_INF = -1e30
MOD_RANK = D_MODEL // 16
SSD_D_INNER = D_MODEL // 2
SSD_HEAD_DIM = 64
SSD_HEADS = SSD_D_INNER // SSD_HEAD_DIM
SSD_GROUPS = 4
SSD_HPG = SSD_HEADS // SSD_GROUPS
SSD_STATE = 128
SSD_CONV = 5
SSD_BC = SSD_GROUPS * SSD_STATE
RET_HEADS = 8
RET_QK_DIM = 128
RET_V_DIM = 2 * RET_QK_DIM
ATTN_HEADS = 16
ATTN_KV_HEADS = 4
ATTN_HEAD_DIM = 128
ATTN_GRP = ATTN_HEADS // ATTN_KV_HEADS
ROPE_DIM = 128
ROPE_BASE = 10000.0
GATE_RANK = D_MODEL // 16
N_GROUPS = 8
EXPERTS_PER_GROUP = 8
N_EXPERTS = N_GROUPS * EXPERTS_PER_GROUP
EXPERT_FF = D_MODEL // 32
SSD_XBC = SSD_D_INNER + 2 * SSD_BC
RET_Q = RET_HEADS * RET_QK_DIM
RET_V = RET_HEADS * RET_V_DIM
ATTN_Q = ATTN_HEADS * ATTN_HEAD_DIM
ATTN_KV = ATTN_KV_HEADS * ATTN_HEAD_DIM
IN_SPLITS = (SSD_D_INNER, SSD_XBC, 2 * SSD_HEADS, RET_Q, RET_Q, RET_V, RET_V, ATTN_Q, ATTN_KV, ATTN_KV, GATE_RANK)

LANES = 128
BF16_SUBLANES = 16
VMEM_LIMIT = 56 * 1024 * 1024

ROPE_AQ = 0
ROPE_RQ = ROPE_AQ + ATTN_Q
ROPE_RK = ROPE_RQ + RET_Q
ROPE_AK = ROPE_RK + RET_Q
ROPE_W = ROPE_AK + ATTN_KV
PL_Z = 0
PL_XS = PL_Z + SSD_D_INNER
PL_RV = PL_XS + SSD_D_INNER
PL_RG = PL_RV + RET_V
PL_AV = PL_RG + RET_V
PL_B = PL_AV + ATTN_KV
PL_C = PL_B + SSD_BC
PL_GD = PL_C + SSD_BC
PLAIN_W = PL_GD + GATE_RANK
PROJ_TN = 768
DT_W = 2 * LANES

MOE_TM = 256
MOE_SPLIT = 2
MOE_SLAB = EXPERTS_PER_GROUP * EXPERT_FF // MOE_SPLIT
ROW_TM = 256


def _cparams(sem):
    return pltpu.CompilerParams(dimension_semantics=sem, vmem_limit_bytes=VMEM_LIMIT)


def _silu(x):
    return x * (1.0 / (1.0 + jnp.exp(-x)))


def _split3(x):
    hi = x.astype(BF16)
    r1 = x - hi.astype(F32)
    mid = r1.astype(BF16)
    lo = (r1 - mid.astype(F32)).astype(BF16)
    return hi, mid, lo


def _dot01_left(m01, x):
    hi, mid, lo = _split3(x)
    d = lambda t: jnp.dot(m01, t, preferred_element_type=F32)
    return d(hi) + d(mid) + d(lo)


def _dot01_right(x, m01):
    hi, mid, lo = _split3(x)
    d = lambda t: jnp.dot(t, m01, preferred_element_type=F32)
    return d(hi) + d(mid) + d(lo)


def _mod_body(c_ref, down_ref, up_ref, b_ref, o_ref):
    s = _silu(c_ref[...])
    r = jnp.dot(s.astype(BF16), down_ref[...].astype(BF16), preferred_element_type=F32)
    o_ref[...] = jnp.dot(r.astype(BF16), up_ref[...].astype(BF16), preferred_element_type=F32) + b_ref[...]


def _modulation(cc, mod_down, mod_up, mod_b):
    depth = mod_down.shape[0]
    d = cc.shape[1]
    return pl.pallas_call(
        _mod_body,
        grid=(depth, 6),
        in_specs=[
            pl.BlockSpec((16, d), lambda l, j: (0, 0)),
            pl.BlockSpec((None, d, MOD_RANK), lambda l, j: (l, 0, 0)),
            pl.BlockSpec((None, MOD_RANK, d), lambda l, j: (l, 0, j)),
            pl.BlockSpec((None, 1, d), lambda l, j: (l, 0, j)),
        ],
        out_specs=pl.BlockSpec((None, 16, d), lambda l, j: (l, 0, j)),
        out_shape=jax.ShapeDtypeStruct((depth, 16, 6 * d), F32),
        compiler_params=_cparams(("arbitrary", "arbitrary")),
        name="modulation",
    )(cc, mod_down, mod_up, mod_b.reshape(depth, 1, 6 * d))


def _mod_row(mod_ref, k, first_row, n_ctx):
    d = D_MODEL
    r = jnp.where(first_row < n_ctx, 0, 1)
    return mod_ref[pl.ds(r, 1), k * d:(k + 1) * d]


def _norm_mod(x, g_ref, mod_ref, k_shift, k_scale, first_row, n_ctx):
    rstd = lax.rsqrt(jnp.mean(x * x, axis=-1, keepdims=True) + NORM_EPS)
    gain = g_ref[...] * (1.0 + _mod_row(mod_ref, k_scale, first_row, n_ctx))
    return x * rstd * gain + _mod_row(mod_ref, k_shift, first_row, n_ctx)


def _is_ctx_rows(first_row, tm, n_ctx):
    return (first_row + lax.broadcasted_iota(I32, (tm, 1), 0)) < n_ctx


def _layer_vec(layer, d):
    return pl.BlockSpec((None, 1, d), lambda *_: (layer, 0, 0))


def _layer_mod(layer, d):
    return pl.BlockSpec((None, 16, 6 * d), lambda *_: (layer, 0, 0))


def _normmod_body(x_ref, g_ref, mod_ref, o_ref, *, n_ctx, tm):
    o_ref[...] = _norm_mod(x_ref[...], g_ref, mod_ref, 0, 1, pl.program_id(0) * tm, n_ctx).astype(o_ref.dtype)


def _normmod(x, g, mods, layer, *, n_ctx, tm=ROW_TM):
    t, d = x.shape
    return pl.pallas_call(
        functools.partial(_normmod_body, n_ctx=n_ctx, tm=tm),
        grid=(t // tm,),
        in_specs=[pl.BlockSpec((tm, d), lambda i: (i, 0)), _layer_vec(layer, d), _layer_mod(layer, d)],
        out_specs=pl.BlockSpec((tm, d), lambda i: (i, 0)),
        out_shape=jax.ShapeDtypeStruct((t, d), BF16),
        compiler_params=_cparams(("parallel",)),
        name="normmod",
    )(x, g, mods)


def _pow2_floor(n):
    return 1 << (max(int(n), 1).bit_length() - 1)


def _cast_job(w, layer, axis, n_blocks, start, steps_per_block, nj):
    _, a, b = w.shape
    rb, cb = (a // n_blocks, b) if axis == 0 else (a, b // n_blocks)

    def blk(i, j):
        return jnp.clip((i * nj + j - start) // steps_per_block, 0, n_blocks - 1)

    if axis == 0:
        in_spec = pl.BlockSpec((None, rb, cb), lambda i, j: (layer, blk(i, j), 0))
        out_spec = pl.BlockSpec((rb, cb), lambda i, j: (blk(i, j), 0))
    else:
        in_spec = pl.BlockSpec((None, rb, cb), lambda i, j: (layer, 0, blk(i, j)))
        out_spec = pl.BlockSpec((rb, cb), lambda i, j: (0, blk(i, j)))
    return w, in_spec, jax.ShapeDtypeStruct((a, b), BF16), out_spec


def _expert_cast_job(w, layer, n_steps, nj, side_by_side):
    _, e, r, c = w.shape
    n_blocks = _pow2_floor(n_steps)
    epb = max(1, e // n_blocks)
    parts = max(1, n_blocks // e)
    n_blocks = (e // epb) * parts
    steps_per_block = n_steps // n_blocks

    def blk(i, j):
        return jnp.clip((i * nj + j) // steps_per_block, 0, n_blocks - 1)

    if side_by_side:
        rp = r // parts
        in_spec = pl.BlockSpec((None, epb, rp, c), lambda i, j: (layer, blk(i, j) // parts, blk(i, j) % parts, 0))
        out_spec = pl.BlockSpec((rp, epb * c), lambda i, j: (blk(i, j) % parts, blk(i, j) // parts))
        return w, in_spec, jax.ShapeDtypeStruct((r, e * c), BF16), out_spec
    cp = c // parts
    in_spec = pl.BlockSpec((None, epb, r, cp), lambda i, j: (layer, blk(i, j) // parts, 0, blk(i, j) % parts))
    out_spec = pl.BlockSpec((epb * r, cp), lambda i, j: (blk(i, j) // parts, blk(i, j) % parts))
    return w, in_spec, jax.ShapeDtypeStruct((e * r, c), BF16), out_spec


def _run_cast_jobs(src_refs, dst_refs):
    for src, dst in zip(src_refs, dst_refs):
        if len(src.shape) == len(dst.shape):
            dst[...] = src[...].astype(dst.dtype)
            continue
        e, r, c = src.shape
        for k in range(e):
            if dst.shape[0] == r:
                dst[:, k * c:(k + 1) * c] = src[k].astype(dst.dtype)
            else:
                dst[k * r:(k + 1) * r, :] = src[k].astype(dst.dtype)


def _proj_rope_body(x_ref, w_ref, cos_ref, sin_ref, cs_ref, *rest, tm, tn):
    n_jobs = (len(rest) - 1) // 2
    o_ref = rest[n_jobs]
    _run_cast_jobs(rest[:n_jobs], rest[n_jobs + 1:])
    acc = jnp.dot(x_ref[...], w_ref[...], preferred_element_type=F32)
    cos = cos_ref[...]
    sin = sin_ref[...]
    lane = lax.broadcasted_iota(I32, (tm, LANES), 1)
    first_half = (lane % (ROPE_DIM // 2)) < (ROPE_DIM // 4)
    for h in range(tn // LANES):
        sl = slice(h * LANES, (h + 1) * LANES)
        a = acc[:, sl]
        rot = jnp.where(first_half, pltpu.roll(a, ROPE_DIM - ROPE_DIM // 4, 1), pltpu.roll(a, ROPE_DIM // 4, 1))
        o_ref[:, sl] = ((a * cos + rot * sin) * cs_ref[:, sl]).astype(o_ref.dtype)


def _mm_body(x_ref, w_ref, *rest):
    n_jobs = (len(rest) - 1) // 2
    o_ref = rest[n_jobs]
    _run_cast_jobs(rest[:n_jobs], rest[n_jobs + 1:])
    o_ref[...] = jnp.dot(x_ref[...], w_ref[...], preferred_element_type=F32).astype(o_ref.dtype)


def _sequential_cast_jobs(layer, n_steps, nj, specs):
    fits = sum(s[2] for s in specs) <= n_steps
    share = _pow2_floor(n_steps // len(specs))
    jobs, start = [], 0
    for w, axis, max_blocks in specs:
        nb = max_blocks if fits else min(max_blocks, share)
        jobs.append(_cast_job(w, layer, axis, nb, start, 1, nj))
        start += nb
    return jobs


def _proj_rope(h, w, layer, cos, sin, colscale, cast_specs, *, tm):
    t, k = h.shape
    n = w.shape[2]
    tn = PROJ_TN
    ni, nj = t // tm, n // tn
    jobs = _sequential_cast_jobs(layer, ni * nj, nj, cast_specs)
    outs = pl.pallas_call(
        functools.partial(_proj_rope_body, tm=tm, tn=tn),
        grid=(ni, nj),
        in_specs=[
            pl.BlockSpec((tm, k), lambda i, j: (i, 0)),
            pl.BlockSpec((None, k, tn), lambda i, j: (layer, 0, j)),
            pl.BlockSpec((tm, LANES), lambda i, j: (i, 0)),
            pl.BlockSpec((tm, LANES), lambda i, j: (i, 0)),
            pl.BlockSpec((1, tn), lambda i, j: (0, j)),
        ] + [jb[1] for jb in jobs],
        out_specs=[pl.BlockSpec((tm, tn), lambda i, j: (i, j))] + [jb[3] for jb in jobs],
        out_shape=[jax.ShapeDtypeStruct((t, n), BF16)] + [jb[2] for jb in jobs],
        compiler_params=_cparams(("arbitrary", "arbitrary")),
        name="proj_rope",
    )(h, w, cos, sin, colscale, *[jb[0] for jb in jobs])
    return outs[0], outs[1:]


def _matmul(x, w, layer, jobs, *, tm, tn, out_dtype, name):
    t, k = x.shape
    n = w.shape[2]
    outs = pl.pallas_call(
        _mm_body,
        grid=(t // tm, n // tn),
        in_specs=[pl.BlockSpec((tm, k), lambda i, j: (i, 0)),
                  pl.BlockSpec((None, k, tn), lambda i, j: (layer, 0, j))] + [jb[1] for jb in jobs],
        out_specs=[pl.BlockSpec((tm, tn), lambda i, j: (i, j))] + [jb[3] for jb in jobs],
        out_shape=[jax.ShapeDtypeStruct((t, n), out_dtype)] + [jb[2] for jb in jobs],
        compiler_params=_cparams(("arbitrary", "arbitrary")),
        name=name,
    )(x, w, *[jb[0] for jb in jobs])
    return outs[0], outs[1:]


def _chunk_of_step(i, *, backward, n_ctx_chunks, n_chunks):
    if not backward:
        return i
    return jnp.where(i < n_ctx_chunks, n_ctx_chunks - 1 - i, n_chunks - 1 - (i - n_ctx_chunks))


def _tri(backward):
    l = lax.broadcasted_iota(I32, (CHUNK, CHUNK), 0)
    s = lax.broadcasted_iota(I32, (CHUNK, CHUNK), 1)
    return (s >= l) if backward else (s <= l)


def _conv_silu(ext_ref, prev_ref, cur_ref, next_ref, w_ref, b_ref, zero_prev, zero_next):
    half = SSD_CONV // 2
    pad = 8
    hp = prev_ref[...].astype(F32)[BF16_SUBLANES - pad:, :]
    hn = next_ref[...].astype(F32)[:pad, :]
    ext_ref[0:pad, :] = hp * jnp.where(zero_prev, 0.0, 1.0)
    ext_ref[pad:pad + CHUNK, :] = cur_ref[...].astype(F32)
    ext_ref[pad + CHUNK:, :] = hn * jnp.where(zero_next, 0.0, 1.0)
    acc = None
    for j in range(SSD_CONV):
        term = ext_ref[pad - half + j:pad - half + j + CHUNK, :] * w_ref[j:j + 1, :]
        acc = term if acc is None else acc + term
    return _silu(acc + b_ref[...])


def _ssd_body(*refs, backward, finalize, n_ctx_chunks, n_chunks):
    if finalize:
        (xs_ref, b_ref, c_ref, dt_ref, dtb_ref, alog_ref, e64_ref, yf_ref, z_ref, dx_ref, ng_ref,
         y_ref, state_ref) = refs
    else:
        (xs_p, xs_c, xs_n, b_p, b_c, b_n, c_p, c_c, c_n, wx_ref, wb_ref, wc_ref, bx_ref, bb_ref, bc_ref,
         dt_ref, dtb_ref, alog_ref, e64_ref,
         y_ref, xs_o, b_o, c_o, state_ref, extx_ref, extb_ref) = refs
    i = pl.program_id(0)
    chunk = _chunk_of_step(i, backward=backward, n_ctx_chunks=n_ctx_chunks, n_chunks=n_chunks)

    @pl.when(i == 0)
    def _():
        state_ref[...] = jnp.zeros_like(state_ref)

    if finalize:
        xs = xs_ref[...].astype(F32)
        bm = b_ref[...]
        cm = c_ref[...]
    else:
        zero_prev = (chunk == 0) | (chunk == n_ctx_chunks)
        zero_next = (chunk == n_ctx_chunks - 1) | (chunk == n_chunks - 1)
        xs = _conv_silu(extx_ref, xs_p, xs_c, xs_n, wx_ref, bx_ref, zero_prev, zero_next)
        bm = _conv_silu(extb_ref, b_p, b_c, b_n, wb_ref, bb_ref, zero_prev, zero_next).astype(BF16)
        cm = _conv_silu(extb_ref, c_p, c_c, c_n, wc_ref, bc_ref, zero_prev, zero_next).astype(BF16)
        xs_o[...] = xs.astype(BF16)
        b_o[...] = bm
        c_o[...] = cm
    xs_b = xs.astype(BF16)

    x = dt_ref[...] + dtb_ref[...]
    dtp = jnp.maximum(x, 0.0) + jnp.log(1.0 + jnp.exp(-jnp.abs(x)))
    la = -jnp.exp(alog_ref[...]) * dtp
    mask = _tri(backward)
    tri01 = jnp.where(mask, 1.0, 0.0).astype(BF16)
    acs = _dot01_left(tri01, la) * math.log2(math.e)
    total = acs[0:1, :] if backward else acs[CHUNK - 1:CHUNK, :]
    src_t = (acs - jnp.log2(dtp)).T
    wd_t = (jnp.exp2(total - acs) * dtp).T
    cdec = jnp.broadcast_to(jnp.exp2(total), (8, LANES))
    cdec_x = _dot01_right(cdec, e64_ref[...])[0:1, :]
    lane = lax.broadcasted_iota(I32, (CHUNK, LANES), 1)
    low = lane < SSD_HEAD_DIM

    y_pairs = []
    for g in range(SSD_GROUPS):
        q_g = cm[:, g * SSD_STATE:(g + 1) * SSD_STATE]
        k_g = bm[:, g * SSD_STATE:(g + 1) * SSD_STATE]
        scores = lax.dot_general(q_g, k_g, (((1,), (1,)), ((), ())), preferred_element_type=F32)
        k_t = k_g.astype(F32).T
        for pr in range(SSD_HPG // 2):
            ha = g * SSD_HPG + 2 * pr
            hb = ha + 1
            pair = ha // 2
            sl = slice(pair * LANES, (pair + 1) * LANES)
            col_a = jnp.broadcast_to(acs[:, ha:ha + 1], (CHUNK, CHUNK))
            col_b = jnp.broadcast_to(acs[:, hb:hb + 1], (CHUNK, CHUNK))
            m_a = scores * jnp.exp2(jnp.where(mask, col_a - src_t[ha:ha + 1, :], -jnp.inf))
            m_b = scores * jnp.exp2(jnp.where(mask, col_b - src_t[hb:hb + 1, :], -jnp.inf))
            s_a = k_t * wd_t[ha:ha + 1, :]
            s_b = k_t * wd_t[hb:hb + 1, :]
            lhs = jnp.concatenate(
                [jnp.concatenate([m_a, m_b], axis=1), jnp.concatenate([s_a, s_b], axis=1)], axis=0).astype(BF16)
            xp = xs_b[:, sl]
            zero = jnp.zeros_like(xp)
            rhs = jnp.concatenate([jnp.where(low, xp, zero), jnp.where(low, zero, xp)], axis=0)
            both = jnp.dot(lhs, rhs, preferred_element_type=F32)
            prev = state_ref[pair]
            inter = jnp.dot(q_g, prev.astype(BF16), preferred_element_type=F32)
            inter = inter * jnp.exp2(jnp.where(low, col_a, col_b))
            y_pair = both[:CHUNK] + inter
            state_ref[pair] = prev * cdec_x[:, sl] + both[CHUNK:]
            if finalize:
                y_pairs.append(y_pair + yf_ref[:, sl] + dx_ref[:, sl] * xs[:, sl])
            else:
                y_ref[:, sl] = y_pair

    if finalize:
        y = jnp.concatenate(y_pairs, axis=1) * _silu(z_ref[...].astype(F32))
        gw = SSD_D_INNER // SSD_GROUPS
        for g in range(SSD_GROUPS):
            sl = slice(g * gw, (g + 1) * gw)
            yg = y[:, sl]
            yg = yg * lax.rsqrt(jnp.mean(yg * yg, axis=-1, keepdims=True) + NORM_EPS)
            y_ref[:, sl] = (yg * ng_ref[:, sl]).astype(y_ref.dtype)


def _dir_vec(layer, direction):
    return pl.BlockSpec((None, None, 1, LANES), lambda i: (layer, direction, 0, 0))


def _ssd_forward(proj, dtraw, conv_w, conv_b, dtb, alog, e64, layer, *, n_ctx):
    t = proj.shape[0]
    n_chunks = t // CHUNK
    ncc = n_ctx // CHUNK
    hb = CHUNK // BF16_SUBLANES
    n_hblk = t // BF16_SUBLANES

    def cur(col, w):
        return pl.BlockSpec((CHUNK, w), lambda i: (i, col // w))

    def prev(col, w):
        return pl.BlockSpec((BF16_SUBLANES, w), lambda i: (jnp.maximum(i * hb - 1, 0), col // w))

    def nxt(col, w):
        return pl.BlockSpec((BF16_SUBLANES, w), lambda i: (jnp.minimum((i + 1) * hb, n_hblk - 1), col // w))

    def cw(rows, col, w):
        return pl.BlockSpec((None, rows, w), lambda i: (layer, 0, col // w))

    in_specs = [
        prev(PL_XS, SSD_D_INNER), cur(PL_XS, SSD_D_INNER), nxt(PL_XS, SSD_D_INNER),
        prev(PL_B, SSD_BC), cur(PL_B, SSD_BC), nxt(PL_B, SSD_BC),
        prev(PL_C, SSD_BC), cur(PL_C, SSD_BC), nxt(PL_C, SSD_BC),
        cw(8, 0, SSD_D_INNER), cw(8, SSD_D_INNER, SSD_BC), cw(8, SSD_D_INNER + SSD_BC, SSD_BC),
        cw(1, 0, SSD_D_INNER), cw(1, SSD_D_INNER, SSD_BC), cw(1, SSD_D_INNER + SSD_BC, SSD_BC),
        pl.BlockSpec((CHUNK, LANES), lambda i: (i, 0)),
        _dir_vec(layer, 0), _dir_vec(layer, 0),
        pl.BlockSpec((LANES, SSD_D_INNER), lambda i: (0, 0)),
    ]
    row = lambda w: pl.BlockSpec((CHUNK, w), lambda i: (i, 0))
    out_shape = [
        jax.ShapeDtypeStruct((t, SSD_D_INNER), F32),
        jax.ShapeDtypeStruct((t, SSD_D_INNER), BF16),
        jax.ShapeDtypeStruct((t, SSD_BC), BF16),
        jax.ShapeDtypeStruct((t, SSD_BC), BF16),
    ]
    return pl.pallas_call(
        functools.partial(_ssd_body, backward=False, finalize=False, n_ctx_chunks=ncc, n_chunks=n_chunks),
        grid=(n_chunks,),
        in_specs=in_specs,
        out_specs=[row(SSD_D_INNER), row(SSD_D_INNER), row(SSD_BC), row(SSD_BC)],
        out_shape=out_shape,
        scratch_shapes=[
            pltpu.VMEM((SSD_HEADS // 2, SSD_STATE, LANES), F32),
            pltpu.VMEM((CHUNK + 16, SSD_D_INNER), F32),
            pltpu.VMEM((CHUNK + 16, SSD_BC), F32),
        ],
        compiler_params=_cparams(("arbitrary",)),
        name="ssd_fwd",
    )(proj, proj, proj, proj, proj, proj, proj, proj, proj,
      conv_w, conv_w, conv_w, conv_b, conv_b, conv_b, dtraw, dtb, alog, e64)


def _ssd_backward(proj, xs, bm, cm, dtraw, dtb, alog, e64, yf, d_x, norm_g, layer, *, n_ctx):
    t = proj.shape[0]
    n_chunks = t // CHUNK
    ncc = n_ctx // CHUNK
    cmap = functools.partial(_chunk_of_step, backward=True, n_ctx_chunks=ncc, n_chunks=n_chunks)
    row = lambda w, col=0: pl.BlockSpec((CHUNK, w), lambda i: (cmap(i), col // w))
    in_specs = [
        row(SSD_D_INNER), row(SSD_BC), row(SSD_BC),
        pl.BlockSpec((CHUNK, LANES), lambda i: (cmap(i), 1)),
        _dir_vec(layer, 1), _dir_vec(layer, 1),
        pl.BlockSpec((LANES, SSD_D_INNER), lambda i: (0, 0)),
        row(SSD_D_INNER), row(SSD_D_INNER, PL_Z), _layer_vec(layer, SSD_D_INNER), _layer_vec(layer, SSD_D_INNER),
    ]
    return pl.pallas_call(
        functools.partial(_ssd_body, backward=True, finalize=True, n_ctx_chunks=ncc, n_chunks=n_chunks),
        grid=(n_chunks,),
        in_specs=in_specs,
        out_specs=row(SSD_D_INNER),
        out_shape=jax.ShapeDtypeStruct((t, SSD_D_INNER), BF16),
        scratch_shapes=[pltpu.VMEM((SSD_HEADS // 2, SSD_STATE, LANES), F32)],
        compiler_params=_cparams(("arbitrary",)),
        name="ssd_bwd",
    )(xs, bm, cm, dtraw, dtb, alog, e64, yf, proj, d_x, norm_g)


def _ret_body(*refs, layer, backward, finalize):
    if finalize:
        dec_ref, q_ref, k_ref, v_ref, yf_ref, g_ref, y_ref, state_ref = refs
    else:
        dec_ref, q_ref, k_ref, v_ref, y_ref, state_ref = refs
    i = pl.program_id(0)

    @pl.when(i == 0)
    def _():
        state_ref[...] = jnp.zeros_like(state_ref)

    mask = _tri(backward)
    l_i = lax.broadcasted_iota(I32, (CHUNK, CHUNK), 0)
    s_i = lax.broadcasted_iota(I32, (CHUNK, CHUNK), 1)
    dist = ((s_i - l_i) if backward else (l_i - s_i)).astype(F32)
    pos = lax.broadcasted_iota(I32, (CHUNK, 1), 0).astype(F32)
    srow = lax.broadcasted_iota(I32, (1, CHUNK), 1).astype(F32)
    steps_in = (CHUNK - pos) if backward else (pos + 1.0)
    steps_out = srow if backward else (CHUNK - 1.0 - srow)
    d = 1 if backward else 0
    for h in range(RET_HEADS):
        a = dec_ref[layer, d, h]
        qh = q_ref[:, h * RET_QK_DIM:(h + 1) * RET_QK_DIM]
        kh = k_ref[:, h * RET_QK_DIM:(h + 1) * RET_QK_DIM]
        vh = v_ref[:, h * RET_V_DIM:(h + 1) * RET_V_DIM]
        scores = lax.dot_general(qh, kh, (((1,), (1,)), ((), ())), preferred_element_type=F32)
        m = scores * jnp.exp(jnp.where(mask, a * dist, -jnp.inf))
        intra = jnp.dot(m.astype(BF16), vh, preferred_element_type=F32)
        prev = state_ref[h]
        inter = jnp.dot(qh, prev.astype(BF16), preferred_element_type=F32) * jnp.exp(a * steps_in)
        kw = (kh.astype(F32).T * jnp.exp(a * steps_out)).astype(BF16)
        carry = jnp.exp(a * jnp.full((1, RET_V_DIM), float(CHUNK), F32))
        state_ref[h] = prev * carry + jnp.dot(kw, vh, preferred_element_type=F32)
        y = intra + inter
        sl = slice(h * RET_V_DIM, (h + 1) * RET_V_DIM)
        if finalize:
            y = y + yf_ref[:, sl]
            y = y - jnp.mean(y, axis=-1, keepdims=True)
            y = y * lax.rsqrt(jnp.mean(y * y, axis=-1, keepdims=True) + NORM_EPS)
            y_ref[:, sl] = (y * _silu(g_ref[:, sl].astype(F32))).astype(y_ref.dtype)
        else:
            y_ref[:, sl] = y


def _retention(proj_r, proj_p, log_decay, yf, layer, *, backward, n_ctx):
    t = proj_r.shape[0]
    n_chunks = t // CHUNK
    ncc = n_ctx // CHUNK
    cmap = functools.partial(_chunk_of_step, backward=backward, n_ctx_chunks=ncc, n_chunks=n_chunks)
    row = lambda w, col=0: pl.BlockSpec((CHUNK, w), lambda i: (cmap(i), col // w))
    in_specs = [pl.BlockSpec(memory_space=pltpu.SMEM), row(RET_Q, ROPE_RQ), row(RET_Q, ROPE_RK), row(RET_V, PL_RV)]
    args = [log_decay, proj_r, proj_r, proj_p]
    if backward:
        in_specs += [row(RET_V), row(RET_V, PL_RG)]
        args += [yf, proj_p]
    return pl.pallas_call(
        functools.partial(_ret_body, layer=layer, backward=backward, finalize=backward),
        grid=(n_chunks,),
        in_specs=in_specs,
        out_specs=row(RET_V),
        out_shape=jax.ShapeDtypeStruct((t, RET_V), BF16 if backward else F32),
        scratch_shapes=[pltpu.VMEM((RET_HEADS, RET_QK_DIM, RET_V_DIM), F32)],
        compiler_params=_cparams(("arbitrary",)),
        name="ret_bwd" if backward else "ret_fwd",
    )(*args)


def _attn_body(sink_ref, q_ref, kc_ref, vc_ref, kp_ref, kq_ref, kn_ref, vp_ref, vq_ref, vn_ref, o_ref,
               *, layer, n_ctx_chunks, n_chunks, n_ctx):
    i = pl.program_id(0)
    is_lat = i >= n_ctx_chunks
    rows = ATTN_GRP * CHUNK
    qi = lax.broadcasted_iota(I32, (rows, CHUNK), 0) % CHUNK
    kj = lax.broadcasted_iota(I32, (rows, CHUNK), 1)
    ok_prev = kj >= qi + jnp.where(i > n_ctx_chunks, 0, CHUNK)
    ok_cur = kj >= jnp.where(is_lat, 0, CHUNK)
    ok_next = kj <= qi - jnp.where(is_lat & (i < n_chunks - 1), 0, CHUNK)
    scale = ATTN_HEAD_DIM ** -0.5
    to_exp2 = scale * math.log2(math.e)
    for hk in range(ATTN_KV_HEADS):
        ksl = slice(hk * ATTN_HEAD_DIM, (hk + 1) * ATTN_HEAD_DIM)
        q = jnp.concatenate(
            [q_ref[:, (hk * ATTN_GRP + g) * ATTN_HEAD_DIM:(hk * ATTN_GRP + g + 1) * ATTN_HEAD_DIM]
             for g in range(ATTN_GRP)], axis=0)
        sink = jnp.concatenate(
            [jnp.full((CHUNK, 1), sink_ref[layer, hk * ATTN_GRP + g] * (1.0 / scale), F32)
             for g in range(ATTN_GRP)], axis=0)
        qk = lambda k: lax.dot_general(q, k, (((1,), (1,)), ((), ())), preferred_element_type=F32)
        s_c = qk(kc_ref[:, ksl])
        segs = [s_c[:, c * LANES:(c + 1) * LANES] for c in range(n_ctx // LANES)]
        segs.append(jnp.where(ok_prev, qk(kp_ref[:, ksl]), NEG_INF))
        segs.append(jnp.where(ok_cur, qk(kq_ref[:, ksl]), NEG_INF))
        segs.append(jnp.where(ok_next, qk(kn_ref[:, ksl]), NEG_INF))
        m_el = functools.reduce(jnp.maximum, segs)
        m = jnp.maximum(jnp.max(m_el, axis=-1, keepdims=True), sink)
        ps = [jnp.exp2((s - m) * to_exp2) for s in segs]
        den = jnp.exp2((sink - m) * to_exp2) + jnp.sum(functools.reduce(jnp.add, ps), axis=-1, keepdims=True)
        nc = n_ctx // LANES
        p_c = jnp.concatenate(ps[:nc], axis=1).astype(BF16)
        pv = lambda p, v: jnp.dot(p, v, preferred_element_type=F32)
        o = (pv(p_c, vc_ref[:, ksl]) + pv(ps[nc].astype(BF16), vp_ref[:, ksl])
             + pv(ps[nc + 1].astype(BF16), vq_ref[:, ksl]) + pv(ps[nc + 2].astype(BF16), vn_ref[:, ksl])) / den
        for g in range(ATTN_GRP):
            h = hk * ATTN_GRP + g
            o_ref[:, h * ATTN_HEAD_DIM:(h + 1) * ATTN_HEAD_DIM] = o[g * CHUNK:(g + 1) * CHUNK].astype(o_ref.dtype)


def _attention(proj_r, proj_p, sink, layer, *, n_ctx):
    t = proj_r.shape[0]
    n_chunks = t // CHUNK
    ncc = n_ctx // CHUNK
    kcol = ROPE_AK // ATTN_KV
    vcol = PL_AV // ATTN_KV
    lat = lambda i, off: jnp.clip(i + off, ncc, n_chunks - 1)
    in_specs = [
        pl.BlockSpec(memory_space=pltpu.SMEM),
        pl.BlockSpec((CHUNK, ATTN_Q), lambda i: (i, ROPE_AQ // ATTN_Q)),
        pl.BlockSpec((n_ctx, ATTN_KV), lambda i: (0, kcol)),
        pl.BlockSpec((n_ctx, ATTN_KV), lambda i: (0, vcol)),
        pl.BlockSpec((CHUNK, ATTN_KV), lambda i: (lat(i, -1), kcol)),
        pl.BlockSpec((CHUNK, ATTN_KV), lambda i: (i, kcol)),
        pl.BlockSpec((CHUNK, ATTN_KV), lambda i: (lat(i, 1), kcol)),
        pl.BlockSpec((CHUNK, ATTN_KV), lambda i: (lat(i, -1), vcol)),
        pl.BlockSpec((CHUNK, ATTN_KV), lambda i: (i, vcol)),
        pl.BlockSpec((CHUNK, ATTN_KV), lambda i: (lat(i, 1), vcol)),
    ]
    return pl.pallas_call(
        functools.partial(_attn_body, layer=layer, n_ctx_chunks=ncc, n_chunks=n_chunks, n_ctx=n_ctx),
        grid=(n_chunks,),
        in_specs=in_specs,
        out_specs=pl.BlockSpec((CHUNK, ATTN_Q), lambda i: (i, 0)),
        out_shape=jax.ShapeDtypeStruct((t, ATTN_Q), BF16),
        compiler_params=_cparams(("parallel",)),
        name="attention",
    )(sink, proj_r, proj_r, proj_p, proj_r, proj_r, proj_r, proj_p, proj_p, proj_p)


def _merge_body(ya_ref, yb_ref, yc_ref, gd_ref, wa_ref, wb_ref, wc_ref, ga_ref, gb_ref, gc_ref, o_ref):
    gd = gd_ref[...]

    def branch(y_ref, w_ref, g_ref):
        gate = jax.nn.sigmoid(jnp.dot(gd, g_ref[...], preferred_element_type=F32))
        return gate * jnp.dot(y_ref[...], w_ref[...], preferred_element_type=F32)

    o_ref[...] = (branch(ya_ref, wa_ref, ga_ref) + branch(yb_ref, wb_ref, gb_ref)
                  + branch(yc_ref, wc_ref, gc_ref)).astype(o_ref.dtype)


def _merge(ya, yb, yc, proj_p, wa, wb, wc, wg, *, tm, tn=512):
    t = ya.shape[0]
    d = wa.shape[1]
    nj = d // tn
    act = lambda w: pl.BlockSpec((tm, w), lambda i, j: (i, 0))
    wsp = lambda k: pl.BlockSpec((k, tn), lambda i, j: (0, j))
    gsp = lambda b: pl.BlockSpec((GATE_RANK, tn), lambda i, j: (0, b * nj + j))
    return pl.pallas_call(
        _merge_body,
        grid=(t // tm, nj),
        in_specs=[act(SSD_D_INNER), act(RET_V), act(ATTN_Q),
                  pl.BlockSpec((tm, GATE_RANK), lambda i, j: (i, PL_GD // GATE_RANK)),
                  wsp(SSD_D_INNER), wsp(RET_V), wsp(ATTN_Q), gsp(0), gsp(1), gsp(2)],
        out_specs=pl.BlockSpec((tm, tn), lambda i, j: (i, j)),
        out_shape=jax.ShapeDtypeStruct((t, d), BF16),
        compiler_params=_cparams(("parallel", "arbitrary")),
        name="merge",
    )(ya, yb, yc, proj_p, wa, wb, wc, wg, wg, wg)


def _outproj_body(m_ref, w_ref, x_ref, mod_ref, *rest, n_ctx, tm):
    n_jobs = (len(rest) - 1) // 2
    o_ref = rest[n_jobs]
    _run_cast_jobs(rest[:n_jobs], rest[n_jobs + 1:])
    acc = jnp.dot(m_ref[...], w_ref[...], preferred_element_type=F32)
    is_ctx = _is_ctx_rows(pl.program_id(0) * tm, tm, n_ctx)
    gate = jnp.where(is_ctx, mod_ref[0:1, :], mod_ref[1:2, :])
    o_ref[...] = x_ref[...] + gate * acc


def _outproj(merged, w, x, mods, layer, moe_ws, *, n_ctx, tm, tn=512):
    t, k = merged.shape
    d = w.shape[1]
    ni, nj = t // tm, d // tn
    k_gate = 2
    jobs = [_expert_cast_job(moe_ws[0], layer, ni * nj, nj, True),
            _expert_cast_job(moe_ws[1], layer, ni * nj, nj, True),
            _expert_cast_job(moe_ws[2], layer, ni * nj, nj, False)]
    outs = pl.pallas_call(
        functools.partial(_outproj_body, n_ctx=n_ctx, tm=tm),
        grid=(ni, nj),
        in_specs=[
            pl.BlockSpec((tm, k), lambda i, j: (i, 0)),
            pl.BlockSpec((k, tn), lambda i, j: (0, j)),
            pl.BlockSpec((tm, tn), lambda i, j: (i, j)),
            pl.BlockSpec((None, 16, tn), lambda i, j: (layer, 0, k_gate * nj + j)),
        ] + [jb[1] for jb in jobs],
        out_specs=[pl.BlockSpec((tm, tn), lambda i, j: (i, j))] + [jb[3] for jb in jobs],
        out_shape=[jax.ShapeDtypeStruct((t, d), F32)] + [jb[2] for jb in jobs],
        compiler_params=_cparams(("arbitrary", "arbitrary")),
        name="outproj",
    )(merged, w, x, mods, *[jb[0] for jb in jobs])
    return outs[0], outs[1:]


def _router_logits(hb, wr_ref, rb_ref):
    return jnp.dot(hb, wr_ref[...], preferred_element_type=F32) + rb_ref[...]


def _router_body(x_ref, g_ref, mod_ref, wr_ref, rb_ref, gi_ref, *, n_ctx, tm):
    h = _norm_mod(x_ref[...], g_ref, mod_ref, 3, 4, pl.program_id(0) * tm, n_ctx)
    logits = _router_logits(h.astype(BF16), wr_ref, rb_ref)
    lane = lax.broadcasted_iota(I32, (tm, LANES), 1)
    is_g = (lane >= N_EXPERTS) & (lane < N_EXPERTS + N_GROUPS)
    gl = jnp.where(is_g, logits, -jnp.inf)
    gmax = jnp.max(gl, axis=-1, keepdims=True)
    first = jnp.min(jnp.where(gl == gmax, (lane - N_EXPERTS).astype(F32), float(N_GROUPS)), axis=-1, keepdims=True)
    gi_ref[...] = first.astype(I32)


def _router(x, g, mods, wr, rb, layer, *, n_ctx, tm=ROW_TM):
    t, d = x.shape
    return pl.pallas_call(
        functools.partial(_router_body, n_ctx=n_ctx, tm=tm),
        grid=(t // tm,),
        in_specs=[
            pl.BlockSpec((tm, d), lambda i: (i, 0)),
            _layer_vec(layer, d), _layer_mod(layer, d),
            pl.BlockSpec((None, d, LANES), lambda i: (layer, 0, 0)),
            _layer_vec(layer, LANES),
        ],
        out_specs=pl.BlockSpec((tm, 1), lambda i: (i, 0)),
        out_shape=jax.ShapeDtypeStruct((t, 1), I32),
        compiler_params=_cparams(("parallel",)),
        name="moe_router",
    )(x, g, mods, wr, rb)


def _positions_body(g_ref, pos_ref, tg_ref, *, n_rows, tm):
    gidx = g_ref[...]
    li = lax.broadcasted_iota(I32, (LANES, LANES), 0)
    lj = lax.broadcasted_iota(I32, (LANES, LANES), 1)
    upper = jnp.where(li <= lj, 1.0, 0.0).astype(BF16)
    ri = lax.broadcasted_iota(I32, (n_rows, n_rows), 0)
    rj = lax.broadcasted_iota(I32, (n_rows, n_rows), 1)
    strict = jnp.where(rj < ri, 1.0, 0.0).astype(BF16)
    tile_start = (lax.broadcasted_iota(I32, (8, LANES), 1) * tm).astype(F32)
    pos = jnp.zeros((n_rows, LANES), F32)
    tg = jnp.zeros((8, LANES), F32)
    off = jnp.zeros((1, 1), F32)
    for g in range(N_GROUPS):
        mk = jnp.where(gidx == g, 1.0, 0.0)
        within = jnp.dot(mk.astype(BF16), upper, preferred_element_type=F32)
        rowtot = jnp.broadcast_to(within[:, LANES - 1:LANES], (n_rows, LANES))
        before = jnp.dot(strict, rowtot.astype(BF16), preferred_element_type=F32)
        pos = pos + mk * (off + before + within - 1.0)
        count = jnp.sum(jnp.sum(mk, axis=-1, keepdims=True), axis=0, keepdims=True)
        padded = jnp.floor((count + (tm - 1.0)) * (1.0 / tm)) * tm
        off = off + padded
        tg = tg + jnp.where(tile_start >= off, 1.0, 0.0)
    pos_ref[...] = pos.astype(I32)
    lane = lax.broadcasted_iota(I32, (8, LANES), 1)
    n_used = off * (1.0 / tm)
    tgi = jnp.minimum(tg, N_GROUPS - 1.0)
    tg_ref[...] = jnp.where(lane == LANES - 1, n_used, tgi).astype(I32)


def _positions(gidx2d, *, tm):
    n_rows = gidx2d.shape[0]
    return pl.pallas_call(
        functools.partial(_positions_body, n_rows=n_rows, tm=tm),
        out_shape=[jax.ShapeDtypeStruct((n_rows, LANES), I32), jax.ShapeDtypeStruct((8, LANES), I32)],
        name="moe_positions",
    )(gidx2d)


ROW_DMA_UNROLL = 8


def _start_then_wait_rows(row_copy, n_rows):
    def start(r, c):
        row_copy(r).start()
        return c

    def wait(r, c):
        row_copy(r).wait()
        return c

    lax.fori_loop(0, n_rows, start, 0, unroll=ROW_DMA_UNROLL)
    lax.fori_loop(0, n_rows, wait, 0, unroll=ROW_DMA_UNROLL)


def _scatter_body(pos_ref, x_ref, g_ref, mod_ref, init_ref, o_ref, h_ref, sem, *, tm, n_ctx):
    del init_ref
    base = pl.program_id(0) * tm
    h_ref[...] = _norm_mod(x_ref[...], g_ref, mod_ref, 3, 4, base, n_ctx)

    def row_copy(r):
        return pltpu.make_async_copy(h_ref.at[pl.ds(r, 1), :], o_ref.at[pl.ds(pos_ref[base + r], 1), :], sem)

    _start_then_wait_rows(row_copy, tm)


def _scatter_rows(pos, x, g, mods, init, layer, *, n_ctx, tm=ROW_TM):
    t, d = x.shape
    return pl.pallas_call(
        functools.partial(_scatter_body, tm=tm, n_ctx=n_ctx),
        grid_spec=pltpu.PrefetchScalarGridSpec(
            num_scalar_prefetch=1,
            grid=(t // tm,),
            in_specs=[pl.BlockSpec((tm, d), lambda i, p: (i, 0)), _layer_vec(layer, d), _layer_mod(layer, d),
                      pl.BlockSpec(memory_space=pl.ANY)],
            out_specs=pl.BlockSpec(memory_space=pl.ANY),
            scratch_shapes=[pltpu.VMEM((tm, d), F32), pltpu.SemaphoreType.DMA(())],
        ),
        out_shape=jax.ShapeDtypeStruct(init.shape, init.dtype),
        input_output_aliases={4: 0},
        compiler_params=_cparams(("arbitrary",)),
        name="moe_scatter",
    )(pos, x, g, mods, init)


def _experts_body(tg_ref, x_ref, wr_ref, rb_ref, w1_ref, w3_ref, w2_ref, o_ref, xb_ref, gate_ref, *, tm):
    ti = pl.program_id(0)
    q = pl.program_id(1)
    grp = tg_ref[ti]
    used = ti < tg_ref[LANES - 1]
    lane = lax.broadcasted_iota(I32, (tm, LANES), 1)
    lane_f = lane.astype(F32)

    @pl.when(jnp.logical_not(used) & (q == 0))
    def _():
        o_ref[...] = jnp.zeros_like(o_ref)

    @pl.when(used & (q == 0))
    def _():
        xb = x_ref[...].astype(BF16)
        xb_ref[...] = xb
        logits = _router_logits(xb, wr_ref, rb_ref)
        is_g = (lane >= N_EXPERTS) & (lane < N_EXPERTS + N_GROUPS)
        gl = jnp.where(is_g, logits, -jnp.inf)
        gmax = jnp.max(gl, axis=-1, keepdims=True)
        gsum = jnp.sum(jnp.exp(gl - gmax), axis=-1, keepdims=True)
        gsel = jnp.sum(jnp.where(lane == N_EXPERTS + grp, logits, 0.0), axis=-1, keepdims=True)
        p_grp = jnp.exp(gsel - gmax) / gsum
        in_grp = (lane >= grp * EXPERTS_PER_GROUP) & (lane < (grp + 1) * EXPERTS_PER_GROUP)
        el = jnp.where(in_grp, logits, -jnp.inf)
        v1 = jnp.max(el, axis=-1, keepdims=True)
        i1 = jnp.min(jnp.where(el == v1, lane_f, float(LANES)), axis=-1, keepdims=True)
        el2 = jnp.where(lane_f == i1, -jnp.inf, el)
        v2 = jnp.max(el2, axis=-1, keepdims=True)
        i2 = jnp.min(jnp.where(el2 == v2, lane_f, float(LANES)), axis=-1, keepdims=True)
        e2 = jnp.exp(v2 - v1)
        w1 = p_grp / (1.0 + e2)
        w2 = p_grp * e2 / (1.0 + e2)
        gate_ref[...] = jnp.where(lane_f == i1, w1, 0.0) + jnp.where(lane_f == i2, w2, 0.0)

    @pl.when(used)
    def _():
        xb = xb_ref[...]
        a = jnp.dot(xb, w1_ref[...], preferred_element_type=F32)
        u = jnp.dot(xb, w3_ref[...], preferred_element_type=F32)
        gates = gate_ref[...]
        per = MOE_SLAB // EXPERT_FF
        cols = []
        for j in range(per):
            e = grp * EXPERTS_PER_GROUP + q * per + j
            ge = jnp.sum(jnp.where(lane == e, gates, 0.0), axis=-1, keepdims=True)
            cols.append(jnp.broadcast_to(ge, (tm, EXPERT_FF)))
        hid = (_silu(a) * u * jnp.concatenate(cols, axis=1)).astype(BF16)
        part = jnp.dot(hid, w2_ref[...], preferred_element_type=F32)

        @pl.when(q == 0)
        def _():
            o_ref[...] = part

        @pl.when(q > 0)
        def _():
            o_ref[...] += part


def _experts(tile_group, xs, wr, rb, w1, w3, w2, layer, *, tm):
    n_rows, d = xs.shape
    n_tiles = n_rows // tm
    return pl.pallas_call(
        functools.partial(_experts_body, tm=tm),
        grid_spec=pltpu.PrefetchScalarGridSpec(
            num_scalar_prefetch=1,
            grid=(n_tiles, MOE_SPLIT),
            in_specs=[
                pl.BlockSpec((tm, d), lambda i, q, tg: (i, 0)),
                pl.BlockSpec((None, d, LANES), lambda i, q, tg: (layer, 0, 0)),
                pl.BlockSpec((None, 1, LANES), lambda i, q, tg: (layer, 0, 0)),
                pl.BlockSpec((d, MOE_SLAB), lambda i, q, tg: (0, tg[i] * MOE_SPLIT + q)),
                pl.BlockSpec((d, MOE_SLAB), lambda i, q, tg: (0, tg[i] * MOE_SPLIT + q)),
                pl.BlockSpec((MOE_SLAB, d), lambda i, q, tg: (tg[i] * MOE_SPLIT + q, 0)),
            ],
            out_specs=pl.BlockSpec((tm, d), lambda i, q, tg: (i, 0)),
            scratch_shapes=[pltpu.VMEM((tm, d), BF16), pltpu.VMEM((tm, LANES), F32)],
        ),
        out_shape=jax.ShapeDtypeStruct((n_rows, d), F32),
        compiler_params=_cparams(("arbitrary", "arbitrary")),
        name="moe_experts",
    )(tile_group, xs, wr, rb, w1, w3, w2)


def _combine_body(pos_ref, x_ref, mod_ref, ys_ref, g_ref, modn_ref, *rest, tm, n_ctx, first_row, final):
    if final:
        o_ref, buf_ref, sem = rest
    else:
        o_ref, h_ref, buf_ref, sem = rest
    base = first_row + pl.program_id(0) * tm

    def row_copy(r):
        return pltpu.make_async_copy(ys_ref.at[pl.ds(pos_ref[base + r], 1), :], buf_ref.at[pl.ds(r, 1), :], sem)

    _start_then_wait_rows(row_copy, tm)
    x = x_ref[...] + _mod_row(mod_ref, 5, base, n_ctx) * buf_ref[...]
    if final:
        o_ref[...] = x * lax.rsqrt(jnp.mean(x * x, axis=-1, keepdims=True) + NORM_EPS) * g_ref[...]
    else:
        o_ref[...] = x
        h_ref[...] = _norm_mod(x, g_ref, modn_ref, 0, 1, base, n_ctx).astype(h_ref.dtype)


def _combine(pos, x, mods, ys, g_next, layer, *, n_ctx, final, tm=ROW_TM):
    t, d = x.shape
    first_tile = n_ctx // tm if final else 0
    n_out = t - first_tile * tm
    next_layer = layer if final else layer + 1
    g_spec = pl.BlockSpec((1, d), lambda i, p: (0, 0)) if final else _layer_vec(next_layer, d)
    out_specs = [pl.BlockSpec((tm, d), lambda i, p: (i, 0))]
    out_shape = [jax.ShapeDtypeStruct((n_out, d), F32)]
    if not final:
        out_specs.append(pl.BlockSpec((tm, d), lambda i, p: (i, 0)))
        out_shape.append(jax.ShapeDtypeStruct((t, d), BF16))
    return pl.pallas_call(
        functools.partial(_combine_body, tm=tm, n_ctx=n_ctx, first_row=first_tile * tm, final=final),
        grid_spec=pltpu.PrefetchScalarGridSpec(
            num_scalar_prefetch=1,
            grid=(n_out // tm,),
            in_specs=[
                pl.BlockSpec((tm, d), lambda i, p: (i + first_tile, 0)),
                _layer_mod(layer, d),
                pl.BlockSpec(memory_space=pl.ANY),
                g_spec,
                _layer_mod(next_layer, d),
            ],
            out_specs=out_specs,
            scratch_shapes=[pltpu.VMEM((tm, d), F32), pltpu.SemaphoreType.DMA(())],
        ),
        out_shape=out_shape,
        compiler_params=_cparams(("arbitrary",)),
        name="moe_combine_final" if final else "moe_combine",
    )(pos, x, mods, ys, g_next, mods)


def _rope_tables(n_ctx, seq):
    n_rows = seq // GRID_W
    row = jnp.repeat(jnp.arange(n_rows), GRID_W).astype(F32)
    col = jnp.tile(jnp.arange(GRID_W), n_rows).astype(F32)
    n_freq = ROPE_DIM // 4
    inv = ROPE_BASE ** (-jnp.arange(n_freq, dtype=F32) / n_freq)
    ang = jnp.concatenate([row[:, None] * inv, row[:, None] * inv, col[:, None] * inv, col[:, None] * inv], axis=-1)
    sign = jnp.where((jnp.arange(ROPE_DIM) % (ROPE_DIM // 2)) < ROPE_DIM // 4, -1.0, 1.0).astype(F32)
    cos = jnp.concatenate([jnp.ones((n_ctx, ROPE_DIM), F32), jnp.cos(ang)], axis=0)
    sin = jnp.concatenate([jnp.zeros((n_ctx, ROPE_DIM), F32), jnp.sin(ang) * sign], axis=0)
    return cos, sin


def _split_w_in(w):
    idx, acc = [], 0
    for s in IN_SPLITS[:-1]:
        acc += s
        idx.append(acc)
    return jnp.split(w, idx, axis=-1)


def _row_tile(t):
    for tm in (1056, 1024, 768, 512, 384, 256):
        if t % tm == 0:
            return tm
    raise ValueError(f"token count {t} is not a multiple of 256")


def kernel(x, c, ctx, c_ctx, norm1_g, norm2_g, mod_down, mod_up, mod_b, w_in, conv_w, conv_b, ssd_a_log, ssd_dt_bias, ssd_d, ssd_norm_g, ret_log_decay, attn_sink, w_branch_a, w_branch_b, w_branch_c, w_gate_up, w_out, router_group_w, router_group_b, router_expert_w, router_expert_b, moe_w1, moe_w3, moe_w2, final_norm_g):
    assert x.shape[0] == 1 and ctx.shape[0] == 1
    seq, d = x.shape[1], x.shape[2]
    n_ctx = ctx.shape[1]
    depth = w_in.shape[0]
    assert d == D_MODEL and seq % ROW_TM == 0 and n_ctx % ROW_TM == 0 and seq % GRID_W == 0
    t = n_ctx + seq
    tm = _row_tile(t)
    tm_merge = 768 if t % 768 == 0 else ROW_TM
    tm_out = tm // 2 if tm % (2 * BF16_SUBLANES) == 0 else tm

    z_w, xbc_w, dt_w, rq_w, rk_w, rv_w, rg_w, aq_w, ak_w, av_w, gd_w = _split_w_in(w_in)
    xs_w, b_w, c_w = jnp.split(xbc_w, [SSD_D_INNER, SSD_D_INNER + SSD_BC], axis=-1)
    w_rope = jnp.concatenate([aq_w, rq_w, rk_w, ak_w], axis=-1).astype(BF16)
    w_plain = jnp.concatenate([z_w, xs_w, rv_w, rg_w, av_w, b_w, c_w, gd_w], axis=-1).astype(BF16)
    zpad = jnp.zeros((depth, d, LANES - SSD_HEADS), F32)
    w_dt = jnp.concatenate([dt_w[..., :SSD_HEADS], zpad, dt_w[..., SSD_HEADS:], zpad], axis=-1).astype(BF16)
    wr = jnp.concatenate([router_expert_w, router_group_w,
                          jnp.zeros((depth, d, LANES - N_EXPERTS - N_GROUPS), F32)], axis=-1).astype(BF16)
    rb = jnp.concatenate([router_expert_b, router_group_b,
                          jnp.zeros((depth, LANES - N_EXPERTS - N_GROUPS), F32)], axis=-1).reshape(depth, 1, LANES)
    conv_w8 = jnp.concatenate([conv_w, jnp.zeros((depth, 8 - SSD_CONV, SSD_XBC), F32)], axis=1)
    conv_b3 = conv_b.reshape(depth, 1, SSD_XBC)
    hpad = jnp.zeros((depth, 2, LANES - SSD_HEADS), F32)
    dtb = jnp.concatenate([ssd_dt_bias, hpad], axis=-1).reshape(depth, 2, 1, LANES)
    alog = jnp.concatenate([ssd_a_log, hpad], axis=-1).reshape(depth, 2, 1, LANES)
    d_x = jnp.repeat(ssd_d, SSD_HEAD_DIM, axis=-1).reshape(depth, 1, SSD_D_INNER)
    ssd_g = ssd_norm_g.reshape(depth, 1, SSD_D_INNER)
    g1 = norm1_g.reshape(depth, 1, d)
    g2 = norm2_g.reshape(depth, 1, d)

    xres = jnp.concatenate([ctx[0], x[0]], axis=0)
    cc = jnp.zeros((16, d), F32).at[0].set(c_ctx).at[1].set(c[0])
    mods = _modulation(cc, mod_down, mod_up, mod_b)
    cos, sin = _rope_tables(n_ctx, seq)
    colscale = jnp.ones((1, ROPE_W), F32).at[:, ROPE_RK:ROPE_RK + RET_Q].set(RET_QK_DIM ** -0.5)
    e64 = (jnp.arange(LANES)[:, None] == (jnp.arange(SSD_D_INNER)[None, :] // SSD_HEAD_DIM)).astype(BF16)
    n_idx_rows = -(-(t // LANES) // 8) * 8
    n_sorted = (t // MOE_TM + N_GROUPS) * MOE_TM
    sorted_buf = jnp.zeros((n_sorted, d), F32)

    h = _normmod(xres, g1, mods, 0, n_ctx=n_ctx)
    out = None
    for i in range(depth):
        proj_r, (wa, wg) = _proj_rope(h, w_rope, i, cos, sin, colscale,
                                      [(w_branch_a, 0, 32), (w_gate_up, 1, 8)], tm=tm)
        nj_plain = PLAIN_W // PROJ_TN
        plain_jobs = _sequential_cast_jobs(i, (t // tm) * nj_plain, nj_plain,
                                           [(w_branch_b, 0, 32), (w_branch_c, 0, 32), (w_out, 0, 32)])
        proj_p, (wb, wc, wo) = _matmul(h, w_plain, i, plain_jobs, tm=tm, tn=PROJ_TN, out_dtype=BF16, name="proj_plain")
        dtraw, _ = _matmul(h, w_dt, i, [], tm=tm, tn=DT_W, out_dtype=F32, name="proj_dt")

        yf, xs_c, b_c, c_c = _ssd_forward(proj_p, dtraw, conv_w8, conv_b3, dtb, alog, e64, i, n_ctx=n_ctx)
        ya = _ssd_backward(proj_p, xs_c, b_c, c_c, dtraw, dtb, alog, e64, yf, d_x, ssd_g, i, n_ctx=n_ctx)
        rf = _retention(proj_r, proj_p, ret_log_decay, None, i, backward=False, n_ctx=n_ctx)
        yb = _retention(proj_r, proj_p, ret_log_decay, rf, i, backward=True, n_ctx=n_ctx)
        yc = _attention(proj_r, proj_p, attn_sink, i, n_ctx=n_ctx)

        merged = _merge(ya, yb, yc, proj_p, wa, wb, wc, wg, tm=tm_merge)
        xres, (w1, w3, w2) = _outproj(merged, wo, xres, mods, i, (moe_w1, moe_w3, moe_w2), n_ctx=n_ctx, tm=tm_out)

        gidx = _router(xres, g2, mods, wr, rb, i, n_ctx=n_ctx)
        gidx2d = jnp.concatenate([gidx[:, 0], jnp.full((n_idx_rows * LANES - t,), N_GROUPS, I32)]).reshape(n_idx_rows, LANES)
        pos2d, tile_group = _positions(gidx2d, tm=MOE_TM)
        pos = pos2d.reshape(-1)[:t]
        sorted_buf = _scatter_rows(pos, xres, g2, mods, sorted_buf, i, n_ctx=n_ctx)
        ys = _experts(tile_group[0], sorted_buf, wr, rb, w1, w3, w2, i, tm=MOE_TM)
        if i + 1 < depth:
            xres, h = _combine(pos, xres, mods, ys, g1, i, n_ctx=n_ctx, final=False)
        else:
            (out,) = _combine(pos, xres, mods, ys, final_norm_g.reshape(1, d), i, n_ctx=n_ctx, final=True)
    return out[None]
```

```python
import functools
import math

import jax
import jax.numpy as jnp
from jax import lax
from jax.experimental import pallas as pl
from jax.experimental.pallas import tpu as pltpu

F32 = jnp.float32
BF16 = jnp.bfloat16
I32 = jnp.int32

D_MODEL = 4096
GRID_W = 64
CHUNK = 128
NORM_EPS = 1e-6
NEG_INF = -1e30
MOD_RANK = D_MODEL // 16
SSD_D_INNER = D_MODEL // 2
SSD_HEAD_DIM = 64
SSD_HEADS = SSD_D_INNER // SSD_HEAD_DIM
SSD_GROUPS = 4
SSD_HPG = SSD_HEADS // SSD_GROUPS
SSD_STATE = 128
SSD_CONV = 5
SSD_BC = SSD_GROUPS * SSD_STATE
RET_HEADS = 8
RET_QK_DIM = 128
RET_V_DIM = 2 * RET_QK_DIM
ATTN_HEADS = 16
ATTN_KV_HEADS = 4
ATTN_HEAD_DIM = 128
ATTN_GRP = ATTN_HEADS // ATTN_KV_HEADS
ROPE_DIM = 128
ROPE_BASE = 10000.0
GATE_RANK = D_MODEL // 16
N_GROUPS = 8
EXPERTS_PER_GROUP = 8
N_EXPERTS = N_GROUPS * EXPERTS_PER_GROUP
EXPERT_FF = D_MODEL // 32
SSD_XBC = SSD_D_INNER + 2 * SSD_BC
RET_Q = RET_HEADS * RET_QK_DIM
RET_V = RET_HEADS * RET_V_DIM
ATTN_Q = ATTN_HEADS * ATTN_HEAD_DIM
ATTN_KV = ATTN_KV_HEADS * ATTN_HEAD_DIM
IN_SPLITS = (SSD_D_INNER, SSD_XBC, 2 * SSD_HEADS, RET_Q, RET_Q, RET_V, RET_V, ATTN_Q, ATTN_KV, ATTN_KV, GATE_RANK)

LANES = 128
BF16_SUBLANES = 16
VMEM_LIMIT = 56 * 1024 * 1024

ROPE_AQ = 0
ROPE_RQ = ROPE_AQ + ATTN_Q
ROPE_RK = ROPE_RQ + RET_Q
ROPE_AK = ROPE_RK + RET_Q
ROPE_W = ROPE_AK + ATTN_KV
PL_Z = 0
PL_XS = PL_Z + SSD_D_INNER
PL_RV = PL_XS + SSD_D_INNER
PL_RG = PL_RV + RET_V
PL_AV = PL_RG + RET_V
PL_B = PL_AV + ATTN_KV
PL_C = PL_B + SSD_BC
PL_GD = PL_C + SSD_BC
PLAIN_W = PL_GD + GATE_RANK
PROJ_TN = 768
DT_W = 2 * LANES

MOE_TM = 256
MOE_SPLIT = 2
MOE_SLAB = EXPERTS_PER_GROUP * EXPERT_FF // MOE_SPLIT
ROW_TM = 256


def _cparams(sem):
    return pltpu.CompilerParams(dimension_semantics=sem, vmem_limit_bytes=VMEM_LIMIT)


def _silu(x):
    return x * (1.0 / (1.0 + jnp.exp(-x)))


def _split3(x):
    hi = x.astype(BF16)
    r1 = x - hi.astype(F32)
    mid = r1.astype(BF16)
    lo = (r1 - mid.astype(F32)).astype(BF16)
    return hi, mid, lo


def _dot01_left(m01, x):
    hi, mid, lo = _split3(x)
    d = lambda t: jnp.dot(m01, t, preferred_element_type=F32)
    return d(hi) + d(mid) + d(lo)


def _dot01_right(x, m01):
    hi, mid, lo = _split3(x)
    d = lambda t: jnp.dot(t, m01, preferred_element_type=F32)
    return d(hi) + d(mid) + d(lo)


def _mod_body(c_ref, down_ref, up_ref, b_ref, o_ref):
    s = _silu(c_ref[...])
    r = jnp.dot(s.astype(BF16), down_ref[...].astype(BF16), preferred_element_type=F32)
    o_ref[...] = jnp.dot(r.astype(BF16), up_ref[...].astype(BF16), preferred_element_type=F32) + b_ref[...]


def _modulation(cc, mod_down, mod_up, mod_b):
    depth = mod_down.shape[0]
    d = cc.shape[1]
    return pl.pallas_call(
        _mod_body,
        grid=(depth, 6),
        in_specs=[
            pl.BlockSpec((16, d), lambda l, j: (0, 0)),
            pl.BlockSpec((None, d, MOD_RANK), lambda l, j: (l, 0, 0)),
            pl.BlockSpec((None, MOD_RANK, d), lambda l, j: (l, 0, j)),
            pl.BlockSpec((None, 1, d), lambda l, j: (l, 0, j)),
        ],
        out_specs=pl.BlockSpec((None, 16, d), lambda l, j: (l, 0, j)),
        out_shape=jax.ShapeDtypeStruct((depth, 16, 6 * d), F32),
        compiler_params=_cparams(("arbitrary", "arbitrary")),
        name="modulation",
    )(cc, mod_down, mod_up, mod_b.reshape(depth, 1, 6 * d))


def _mod_row(mod_ref, k, first_row, n_ctx):
    d = D_MODEL
    r = jnp.where(first_row < n_ctx, 0, 1)
    return mod_ref[pl.ds(r, 1), k * d:(k + 1) * d]


def _norm_mod(x, g_ref, mod_ref, k_shift, k_scale, first_row, n_ctx):
    rstd = lax.rsqrt(jnp.mean(x * x, axis=-1, keepdims=True) + NORM_EPS)
    gain = g_ref[...] * (1.0 + _mod_row(mod_ref, k_scale, first_row, n_ctx))
    return x * rstd * gain + _mod_row(mod_ref, k_shift, first_row, n_ctx)


def _is_ctx_rows(first_row, tm, n_ctx):
    return (first_row + lax.broadcasted_iota(I32, (tm, 1), 0)) < n_ctx


def _layer_vec(layer, d):
    return pl.BlockSpec((None, 1, d), lambda *_: (layer, 0, 0))


def _layer_mod(layer, d):
    return pl.BlockSpec((None, 16, 6 * d), lambda *_: (layer, 0, 0))


def _normmod_body(x_ref, g_ref, mod_ref, o_ref, *, n_ctx, tm):
    o_ref[...] = _norm_mod(x_ref[...], g_ref, mod_ref, 0, 1, pl.program_id(0) * tm, n_ctx).astype(o_ref.dtype)


def _normmod(x, g, mods, layer, *, n_ctx, tm=ROW_TM):
    t, d = x.shape
    return pl.pallas_call(
        functools.partial(_normmod_body, n_ctx=n_ctx, tm=tm),
        grid=(t // tm,),
        in_specs=[pl.BlockSpec((tm, d), lambda i: (i, 0)), _layer_vec(layer, d), _layer_mod(layer, d)],
        out_specs=pl.BlockSpec((tm, d), lambda i: (i, 0)),
        out_shape=jax.ShapeDtypeStruct((t, d), BF16),
        compiler_params=_cparams(("parallel",)),
        name="normmod",
    )(x, g, mods)


def _pow2_floor(n):
    return 1 << (max(int(n), 1).bit_length() - 1)


def _cast_job(w, layer, axis, n_blocks, start, steps_per_block, nj):
    _, a, b = w.shape
    rb, cb = (a // n_blocks, b) if axis == 0 else (a, b // n_blocks)

    def blk(i, j):
        return jnp.clip((i * nj + j - start) // steps_per_block, 0, n_blocks - 1)

    if axis == 0:
        in_spec = pl.BlockSpec((None, rb, cb), lambda i, j: (layer, blk(i, j), 0))
        out_spec = pl.BlockSpec((rb, cb), lambda i, j: (blk(i, j), 0))
    else:
        in_spec = pl.BlockSpec((None, rb, cb), lambda i, j: (layer, 0, blk(i, j)))
        out_spec = pl.BlockSpec((rb, cb), lambda i, j: (0, blk(i, j)))
    return w, in_spec, jax.ShapeDtypeStruct((a, b), BF16), out_spec


def _expert_cast_job(w, layer, n_steps, nj, side_by_side):
    _, e, r, c = w.shape
    n_blocks = _pow2_floor(n_steps)
    epb = max(1, e // n_blocks)
    parts = max(1, n_blocks // e)
    n_blocks = (e // epb) * parts
    steps_per_block = n_steps // n_blocks

    def blk(i, j):
        return jnp.clip((i * nj + j) // steps_per_block, 0, n_blocks - 1)

    if side_by_side:
        rp = r // parts
        in_spec = pl.BlockSpec((None, epb, rp, c), lambda i, j: (layer, blk(i, j) // parts, blk(i, j) % parts, 0))
        out_spec = pl.BlockSpec((rp, epb * c), lambda i, j: (blk(i, j) % parts, blk(i, j) // parts))
        return w, in_spec, jax.ShapeDtypeStruct((r, e * c), BF16), out_spec
    cp = c // parts
    in_spec = pl.BlockSpec((None, epb, r, cp), lambda i, j: (layer, blk(i, j) // parts, 0, blk(i, j) % parts))
    out_spec = pl.BlockSpec((epb * r, cp), lambda i, j: (blk(i, j) // parts, blk(i, j) % parts))
    return w, in_spec, jax.ShapeDtypeStruct((e * r, c), BF16), out_spec


def _run_cast_jobs(src_refs, dst_refs):
    for src, dst in zip(src_refs, dst_refs):
        if len(src.shape) == len(dst.shape):
            dst[...] = src[...].astype(dst.dtype)
            continue
        e, r, c = src.shape
        for k in range(e):
            if dst.shape[0] == r:
                dst[:, k * c:(k + 1) * c] = src[k].astype(dst.dtype)
            else:
                dst[k * r:(k + 1) * r, :] = src[k].astype(dst.dtype)


def _proj_rope_body(x_ref, w_ref, cos_ref, sin_ref, cs_ref, *rest, tm, tn):
    n_jobs = (len(rest) - 1) // 2
    o_ref = rest[n_jobs]
    _run_cast_jobs(rest[:n_jobs], rest[n_jobs + 1:])
    acc = jnp.dot(x_ref[...], w_ref[...], preferred_element_type=F32)
    cos = cos_ref[...]
    sin = sin_ref[...]
    lane = lax.broadcasted_iota(I32, (tm, LANES), 1)
    first_half = (lane % (ROPE_DIM // 2)) < (ROPE_DIM // 4)
    for h in range(tn // LANES):
        sl = slice(h * LANES, (h + 1) * LANES)
        a = acc[:, sl]
        rot = jnp.where(first_half, pltpu.roll(a, ROPE_DIM - ROPE_DIM // 4, 1), pltpu.roll(a, ROPE_DIM // 4, 1))
        o_ref[:, sl] = ((a * cos + rot * sin) * cs_ref[:, sl]).astype(o_ref.dtype)


def _mm_body(x_ref, w_ref, *rest):
    n_jobs = (len(rest) - 1) // 2
    o_ref = rest[n_jobs]
    _run_cast_jobs(rest[:n_jobs], rest[n_jobs + 1:])
    o_ref[...] = jnp.dot(x_ref[...], w_ref[...], preferred_element_type=F32).astype(o_ref.dtype)


def _sequential_cast_jobs(layer, n_steps, nj, specs):
    fits = sum(s[2] for s in specs) <= n_steps
    share = _pow2_floor(n_steps // len(specs))
    jobs, start = [], 0
    for w, axis, max_blocks in specs:
        nb = max_blocks if fits else min(max_blocks, share)
        jobs.append(_cast_job(w, layer, axis, nb, start, 1, nj))
        start += nb
    return jobs


def _proj_rope(h, w, layer, cos, sin, colscale, cast_specs, *, tm):
    t, k = h.shape
    n = w.shape[2]
    tn = PROJ_TN
    ni, nj = t // tm, n // tn
    jobs = _sequential_cast_jobs(layer, ni * nj, nj, cast_specs)
    outs = pl.pallas_call(
        functools.partial(_proj_rope_body, tm=tm, tn=tn),
        grid=(ni, nj),
        in_specs=[
            pl.BlockSpec((tm, k), lambda i, j: (i, 0)),
            pl.BlockSpec((None, k, tn), lambda i, j: (layer, 0, j)),
            pl.BlockSpec((tm, LANES), lambda i, j: (i, 0)),
            pl.BlockSpec((tm, LANES), lambda i, j: (i, 0)),
            pl.BlockSpec((1, tn), lambda i, j: (0, j)),
        ] + [jb[1] for jb in jobs],
        out_specs=[pl.BlockSpec((tm, tn), lambda i, j: (i, j))] + [jb[3] for jb in jobs],
        out_shape=[jax.ShapeDtypeStruct((t, n), BF16)] + [jb[2] for jb in jobs],
        compiler_params=_cparams(("arbitrary", "arbitrary")),
        name="proj_rope",
    )(h, w, cos, sin, colscale, *[jb[0] for jb in jobs])
    return outs[0], outs[1:]


def _matmul(x, w, layer, jobs, *, tm, tn, out_dtype, name):
    t, k = x.shape
    n = w.shape[2]
    outs = pl.pallas_call(
        _mm_body,
        grid=(t // tm, n // tn),
        in_specs=[pl.BlockSpec((tm, k), lambda i, j: (i, 0)),
                  pl.BlockSpec((None, k, tn), lambda i, j: (layer, 0, j))] + [jb[1] for jb in jobs],
        out_specs=[pl.BlockSpec((tm, tn), lambda i, j: (i, j))] + [jb[3] for jb in jobs],
        out_shape=[jax.ShapeDtypeStruct((t, n), out_dtype)] + [jb[2] for jb in jobs],
        compiler_params=_cparams(("arbitrary", "arbitrary")),
        name=name,
    )(x, w, *[jb[0] for jb in jobs])
    return outs[0], outs[1:]


def _chunk_of_step(i, *, backward, n_ctx_chunks, n_chunks):
    if not backward:
        return i
    return jnp.where(i < n_ctx_chunks, n_ctx_chunks - 1 - i, n_chunks - 1 - (i - n_ctx_chunks))


def _tri(backward):
    l = lax.broadcasted_iota(I32, (CHUNK, CHUNK), 0)
    s = lax.broadcasted_iota(I32, (CHUNK, CHUNK), 1)
    return (s >= l) if backward else (s <= l)


def _conv_silu(ext_ref, prev_ref, cur_ref, next_ref, w_ref, b_ref, zero_prev, zero_next):
    half = SSD_CONV // 2
    pad = 8
    hp = prev_ref[...].astype(F32)[BF16_SUBLANES - pad:, :]
    hn = next_ref[...].astype(F32)[:pad, :]
    ext_ref[0:pad, :] = hp * jnp.where(zero_prev, 0.0, 1.0)
    ext_ref[pad:pad + CHUNK, :] = cur_ref[...].astype(F32)
    ext_ref[pad + CHUNK:, :] = hn * jnp.where(zero_next, 0.0, 1.0)
    acc = None
    for j in range(SSD_CONV):
        term = ext_ref[pad - half + j:pad - half + j + CHUNK, :] * w_ref[j:j + 1, :]
        acc = term if acc is None else acc + term
    return _silu(acc + b_ref[...])


def _ssd_body(*refs, backward, finalize, n_ctx_chunks, n_chunks):
    if finalize:
        (xs_ref, b_ref, c_ref, dt_ref, dtb_ref, alog_ref, e64_ref, yf_ref, z_ref, dx_ref, ng_ref,
         y_ref, state_ref) = refs
    else:
        (xs_p, xs_c, xs_n, b_p, b_c, b_n, c_p, c_c, c_n, wx_ref, wb_ref, wc_ref, bx_ref, bb_ref, bc_ref,
         dt_ref, dtb_ref, alog_ref, e64_ref,
         y_ref, xs_o, b_o, c_o, state_ref, extx_ref, extb_ref) = refs
    i = pl.program_id(0)
    chunk = _chunk_of_step(i, backward=backward, n_ctx_chunks=n_ctx_chunks, n_chunks=n_chunks)

    @pl.when(i == 0)
    def _():
        state_ref[...] = jnp.zeros_like(state_ref)

    if finalize:
        xs = xs_ref[...].astype(F32)
        bm = b_ref[...]
        cm = c_ref[...]
    else:
        zero_prev = (chunk == 0) | (chunk == n_ctx_chunks)
        zero_next = (chunk == n_ctx_chunks - 1) | (chunk == n_chunks - 1)
        xs = _conv_silu(extx_ref, xs_p, xs_c, xs_n, wx_ref, bx_ref, zero_prev, zero_next)
        bm = _conv_silu(extb_ref, b_p, b_c, b_n, wb_ref, bb_ref, zero_prev, zero_next).astype(BF16)
        cm = _conv_silu(extb_ref, c_p, c_c, c_n, wc_ref, bc_ref, zero_prev, zero_next).astype(BF16)
        xs_o[...] = xs.astype(BF16)
        b_o[...] = bm
        c_o[...] = cm
    xs_b = xs.astype(BF16)

    x = dt_ref[...] + dtb_ref[...]
    dtp = jnp.maximum(x, 0.0) + jnp.log(1.0 + jnp.exp(-jnp.abs(x)))
    la = -jnp.exp(alog_ref[...]) * dtp
    mask = _tri(backward)
    tri01 = jnp.where(mask, 1.0, 0.0).astype(BF16)
    acs = _dot01_left(tri01, la) * math.log2(math.e)
    total = acs[0:1, :] if backward else acs[CHUNK - 1:CHUNK, :]
    src_t = (acs - jnp.log2(dtp)).T
    wd_t = (jnp.exp2(total - acs) * dtp).T
    cdec = jnp.broadcast_to(jnp.exp2(total), (8, LANES))
    cdec_x = _dot01_right(cdec, e64_ref[...])[0:1, :]
    lane = lax.broadcasted_iota(I32, (CHUNK, LANES), 1)
    low = lane < SSD_HEAD_DIM

    y_pairs = []
    for g in range(SSD_GROUPS):
        q_g = cm[:, g * SSD_STATE:(g + 1) * SSD_STATE]
        k_g = bm[:, g * SSD_STATE:(g + 1) * SSD_STATE]
        scores = lax.dot_general(q_g, k_g, (((1,), (1,)), ((), ())), preferred_element_type=F32)
        k_t = k_g.astype(F32).T
        for pr in range(SSD_HPG // 2):
            ha = g * SSD_HPG + 2 * pr
            hb = ha + 1
            pair = ha // 2
            sl = slice(pair * LANES, (pair + 1) * LANES)
            col_a = jnp.broadcast_to(acs[:, ha:ha + 1], (CHUNK, CHUNK))
            col_b = jnp.broadcast_to(acs[:, hb:hb + 1], (CHUNK, CHUNK))
            m_a = scores * jnp.exp2(jnp.where(mask, col_a - src_t[ha:ha + 1, :], -jnp.inf))
            m_b = scores * jnp.exp2(jnp.where(mask, col_b - src_t[hb:hb + 1, :], -jnp.inf))
            s_a = k_t * wd_t[ha:ha + 1, :]
            s_b = k_t * wd_t[hb:hb + 1, :]
            lhs = jnp.concatenate(
                [jnp.concatenate([m_a, m_b], axis=1), jnp.concatenate([s_a, s_b], axis=1)], axis=0).astype(BF16)
            xp = xs_b[:, sl]
            zero = jnp.zeros_like(xp)
            rhs = jnp.concatenate([jnp.where(low, xp, zero), jnp.where(low, zero, xp)], axis=0)
            both = jnp.dot(lhs, rhs, preferred_element_type=F32)
            prev = state_ref[pair]
            inter = jnp.dot(q_g, prev.astype(BF16), preferred_element_type=F32)
            inter = inter * jnp.exp2(jnp.where(low, col_a, col_b))
            y_pair = both[:CHUNK] + inter
            state_ref[pair] = prev * cdec_x[:, sl] + both[CHUNK:]
            if finalize:
                y_pairs.append(y_pair + yf_ref[:, sl] + dx_ref[:, sl] * xs[:, sl])
            else:
                y_ref[:, sl] = y_pair

    if finalize:
        y = jnp.concatenate(y_pairs, axis=1) * _silu(z_ref[...].astype(F32))
        gw = SSD_D_INNER // SSD_GROUPS
        for g in range(SSD_GROUPS):
            sl = slice(g * gw, (g + 1) * gw)
            yg = y[:, sl]
            yg = yg * lax.rsqrt(jnp.mean(yg * yg, axis=-1, keepdims=True) + NORM_EPS)
            y_ref[:, sl] = (yg * ng_ref[:, sl]).astype(y_ref.dtype)


def _dir_vec(layer, direction):
    return pl.BlockSpec((None, None, 1, LANES), lambda i: (layer, direction, 0, 0))


def _ssd_forward(proj, dtraw, conv_w, conv_b, dtb, alog, e64, layer, *, n_ctx):
    t = proj.shape[0]
    n_chunks = t // CHUNK
    ncc = n_ctx // CHUNK
    hb = CHUNK // BF16_SUBLANES
    n_hblk = t // BF16_SUBLANES

    def cur(col, w):
        return pl.BlockSpec((CHUNK, w), lambda i: (i, col // w))

    def prev(col, w):
        return pl.BlockSpec((BF16_SUBLANES, w), lambda i: (jnp.maximum(i * hb - 1, 0), col // w))

    def nxt(col, w):
        return pl.BlockSpec((BF16_SUBLANES, w), lambda i: (jnp.minimum((i + 1) * hb, n_hblk - 1), col // w))

    def cw(rows, col, w):
        return pl.BlockSpec((None, rows, w), lambda i: (layer, 0, col // w))

    in_specs = [
        prev(PL_XS, SSD_D_INNER), cur(PL_XS, SSD_D_INNER), nxt(PL_XS, SSD_D_INNER),
        prev(PL_B, SSD_BC), cur(PL_B, SSD_BC), nxt(PL_B, SSD_BC),
        prev(PL_C, SSD_BC), cur(PL_C, SSD_BC), nxt(PL_C, SSD_BC),
        cw(8, 0, SSD_D_INNER), cw(8, SSD_D_INNER, SSD_BC), cw(8, SSD_D_INNER + SSD_BC, SSD_BC),
        cw(1, 0, SSD_D_INNER), cw(1, SSD_D_INNER, SSD_BC), cw(1, SSD_D_INNER + SSD_BC, SSD_BC),
        pl.BlockSpec((CHUNK, LANES), lambda i: (i, 0)),
        _dir_vec(layer, 0), _dir_vec(layer, 0),
        pl.BlockSpec((LANES, SSD_D_INNER), lambda i: (0, 0)),
    ]
    row = lambda w: pl.BlockSpec((CHUNK, w), lambda i: (i, 0))
    out_shape = [
        jax.ShapeDtypeStruct((t, SSD_D_INNER), F32),
        jax.ShapeDtypeStruct((t, SSD_D_INNER), BF16),
        jax.ShapeDtypeStruct((t, SSD_BC), BF16),
        jax.ShapeDtypeStruct((t, SSD_BC), BF16),
    ]
    return pl.pallas_call(
        functools.partial(_ssd_body, backward=False, finalize=False, n_ctx_chunks=ncc, n_chunks=n_chunks),
        grid=(n_chunks,),
        in_specs=in_specs,
        out_specs=[row(SSD_D_INNER), row(SSD_D_INNER), row(SSD_BC), row(SSD_BC)],
        out_shape=out_shape,
        scratch_shapes=[
            pltpu.VMEM((SSD_HEADS // 2, SSD_STATE, LANES), F32),
            pltpu.VMEM((CHUNK + 16, SSD_D_INNER), F32),
            pltpu.VMEM((CHUNK + 16, SSD_BC), F32),
        ],
        compiler_params=_cparams(("arbitrary",)),
        name="ssd_fwd",
    )(proj, proj, proj, proj, proj, proj, proj, proj, proj,
      conv_w, conv_w, conv_w, conv_b, conv_b, conv_b, dtraw, dtb, alog, e64)


def _ssd_backward(proj, xs, bm, cm, dtraw, dtb, alog, e64, yf, d_x, norm_g, layer, *, n_ctx):
    t = proj.shape[0]
    n_chunks = t // CHUNK
    ncc = n_ctx // CHUNK
    cmap = functools.partial(_chunk_of_step, backward=True, n_ctx_chunks=ncc, n_chunks=n_chunks)
    row = lambda w, col=0: pl.BlockSpec((CHUNK, w), lambda i: (cmap(i), col // w))
    in_specs = [
        row(SSD_D_INNER), row(SSD_BC), row(SSD_BC),
        pl.BlockSpec((CHUNK, LANES), lambda i: (cmap(i), 1)),
        _dir_vec(layer, 1), _dir_vec(layer, 1),
        pl.BlockSpec((LANES, SSD_D_INNER), lambda i: (0, 0)),
        row(SSD_D_INNER), row(SSD_D_INNER, PL_Z), _layer_vec(layer, SSD_D_INNER), _layer_vec(layer, SSD_D_INNER),
    ]
    return pl.pallas_call(
        functools.partial(_ssd_body, backward=True, finalize=True, n_ctx_chunks=ncc, n_chunks=n_chunks),
        grid=(n_chunks,),
        in_specs=in_specs,
        out_specs=row(SSD_D_INNER),
        out_shape=jax.ShapeDtypeStruct((t, SSD_D_INNER), BF16),
        scratch_shapes=[pltpu.VMEM((SSD_HEADS // 2, SSD_STATE, LANES), F32)],
        compiler_params=_cparams(("arbitrary",)),
        name="ssd_bwd",
    )(xs, bm, cm, dtraw, dtb, alog, e64, yf, proj, d_x, norm_g)


def _ret_body(*refs, layer, backward, finalize):
    if finalize:
        dec_ref, q_ref, k_ref, v_ref, yf_ref, g_ref, y_ref, state_ref = refs
    else:
        dec_ref, q_ref, k_ref, v_ref, y_ref, state_ref = refs
    i = pl.program_id(0)

    @pl.when(i == 0)
    def _():
        state_ref[...] = jnp.zeros_like(state_ref)

    mask = _tri(backward)
    l_i = lax.broadcasted_iota(I32, (CHUNK, CHUNK), 0)
    s_i = lax.broadcasted_iota(I32, (CHUNK, CHUNK), 1)
    dist = ((s_i - l_i) if backward else (l_i - s_i)).astype(F32)
    pos = lax.broadcasted_iota(I32, (CHUNK, 1), 0).astype(F32)
    srow = lax.broadcasted_iota(I32, (1, CHUNK), 1).astype(F32)
    steps_in = (CHUNK - pos) if backward else (pos + 1.0)
    steps_out = srow if backward else (CHUNK - 1.0 - srow)
    d = 1 if backward else 0
    for h in range(RET_HEADS):
        a = dec_ref[layer, d, h]
        qh = q_ref[:, h * RET_QK_DIM:(h + 1) * RET_QK_DIM]
        kh = k_ref[:, h * RET_QK_DIM:(h + 1) * RET_QK_DIM]
        vh = v_ref[:, h * RET_V_DIM:(h + 1) * RET_V_DIM]
        scores = lax.dot_general(qh, kh, (((1,), (1,)), ((), ())), preferred_element_type=F32)
        m = scores * jnp.exp(jnp.where(mask, a * dist, -jnp.inf))
        intra = jnp.dot(m.astype(BF16), vh, preferred_element_type=F32)
        prev = state_ref[h]
        inter = jnp.dot(qh, prev.astype(BF16), preferred_element_type=F32) * jnp.exp(a * steps_in)
        kw = (kh.astype(F32).T * jnp.exp(a * steps_out)).astype(BF16)
        carry = jnp.exp(a * jnp.full((1, RET_V_DIM), float(CHUNK), F32))
        state_ref[h] = prev * carry + jnp.dot(kw, vh, preferred_element_type=F32)
        y = intra + inter
        sl = slice(h * RET_V_DIM, (h + 1) * RET_V_DIM)
        if finalize:
            y = y + yf_ref[:, sl]
            y = y - jnp.mean(y, axis=-1, keepdims=True)
            y = y * lax.rsqrt(jnp.mean(y * y, axis=-1, keepdims=True) + NORM_EPS)
            y_ref[:, sl] = (y * _silu(g_ref[:, sl].astype(F32))).astype(y_ref.dtype)
        else:
            y_ref[:, sl] = y


def _retention(proj_r, proj_p, log_decay, yf, layer, *, backward, n_ctx):
    t = proj_r.shape[0]
    n_chunks = t // CHUNK
    ncc = n_ctx // CHUNK
    cmap = functools.partial(_chunk_of_step, backward=backward, n_ctx_chunks=ncc, n_chunks=n_chunks)
    row = lambda w, col=0: pl.BlockSpec((CHUNK, w), lambda i: (cmap(i), col // w))
    in_specs = [pl.BlockSpec(memory_space=pltpu.SMEM), row(RET_Q, ROPE_RQ), row(RET_Q, ROPE_RK), row(RET_V, PL_RV)]
    args = [log_decay, proj_r, proj_r, proj_p]
    if backward:
        in_specs += [row(RET_V), row(RET_V, PL_RG)]
        args += [yf, proj_p]
    return pl.pallas_call(
        functools.partial(_ret_body, layer=layer, backward=backward, finalize=backward),
        grid=(n_chunks,),
        in_specs=in_specs,
        out_specs=row(RET_V),
        out_shape=jax.ShapeDtypeStruct((t, RET_V), BF16 if backward else F32),
        scratch_shapes=[pltpu.VMEM((RET_HEADS, RET_QK_DIM, RET_V_DIM), F32)],
        compiler_params=_cparams(("arbitrary",)),
        name="ret_bwd" if backward else "ret_fwd",
    )(*args)


def _attn_body(sink_ref, q_ref, kc_ref, vc_ref, kp_ref, kq_ref, kn_ref, vp_ref, vq_ref, vn_ref, o_ref,
               *, layer, n_ctx_chunks, n_chunks, n_ctx):
    i = pl.program_id(0)
    is_lat = i >= n_ctx_chunks
    rows = ATTN_GRP * CHUNK
    qi = lax.broadcasted_iota(I32, (rows, CHUNK), 0) % CHUNK
    kj = lax.broadcasted_iota(I32, (rows, CHUNK), 1)
    ok_prev = kj >= qi + jnp.where(i > n_ctx_chunks, 0, CHUNK)
    ok_cur = kj >= jnp.where(is_lat, 0, CHUNK)
    ok_next = kj <= qi - jnp.where(is_lat & (i < n_chunks - 1), 0, CHUNK)
    scale = ATTN_HEAD_DIM ** -0.5
    to_exp2 = scale * math.log2(math.e)
    for hk in range(ATTN_KV_HEADS):
        ksl = slice(hk * ATTN_HEAD_DIM, (hk + 1) * ATTN_HEAD_DIM)
        q = jnp.concatenate(
            [q_ref[:, (hk * ATTN_GRP + g) * ATTN_HEAD_DIM:(hk * ATTN_GRP + g + 1) * ATTN_HEAD_DIM]
             for g in range(ATTN_GRP)], axis=0)
        sink = jnp.concatenate(
            [jnp.full((CHUNK, 1), sink_ref[layer, hk * ATTN_GRP + g] * (1.0 / scale), F32)
             for g in range(ATTN_GRP)], axis=0)
        qk = lambda k: lax.dot_general(q, k, (((1,), (1,)), ((), ())), preferred_element_type=F32)
        s_c = qk(kc_ref[:, ksl])
        segs = [s_c[:, c * LANES:(c + 1) * LANES] for c in range(n_ctx // LANES)]
        segs.append(jnp.where(ok_prev, qk(kp_ref[:, ksl]), NEG_INF))
        segs.append(jnp.where(ok_cur, qk(kq_ref[:, ksl]), NEG_INF))
        segs.append(jnp.where(ok_next, qk(kn_ref[:, ksl]), NEG_INF))
        m_el = functools.reduce(jnp.maximum, segs)
        m = jnp.maximum(jnp.max(m_el, axis=-1, keepdims=True), sink)
        ps = [jnp.exp2((s - m) * to_exp2) for s in segs]
        den = jnp.exp2((sink - m) * to_exp2) + jnp.sum(functools.reduce(jnp.add, ps), axis=-1, keepdims=True)
        nc = n_ctx // LANES
        p_c = jnp.concatenate(ps[:nc], axis=1).astype(BF16)
        pv = lambda p, v: jnp.dot(p, v, preferred_element_type=F32)
        o = (pv(p_c, vc_ref[:, ksl]) + pv(ps[nc].astype(BF16), vp_ref[:, ksl])
             + pv(ps[nc + 1].astype(BF16), vq_ref[:, ksl]) + pv(ps[nc + 2].astype(BF16), vn_ref[:, ksl])) / den
        for g in range(ATTN_GRP):
            h = hk * ATTN_GRP + g
            o_ref[:, h * ATTN_HEAD_DIM:(h + 1) * ATTN_HEAD_DIM] = o[g * CHUNK:(g + 1) * CHUNK].astype(o_ref.dtype)


def _attention(proj_r, proj_p, sink, layer, *, n_ctx):
    t = proj_r.shape[0]
    n_chunks = t // CHUNK
    ncc = n_ctx // CHUNK
    kcol = ROPE_AK // ATTN_KV
    vcol = PL_AV // ATTN_KV
    lat = lambda i, off: jnp.clip(i + off, ncc, n_chunks - 1)
    in_specs = [
        pl.BlockSpec(memory_space=pltpu.SMEM),
        pl.BlockSpec((CHUNK, ATTN_Q), lambda i: (i, ROPE_AQ // ATTN_Q)),
        pl.BlockSpec((n_ctx, ATTN_KV), lambda i: (0, kcol)),
        pl.BlockSpec((n_ctx, ATTN_KV), lambda i: (0, vcol)),
        pl.BlockSpec((CHUNK, ATTN_KV), lambda i: (lat(i, -1), kcol)),
        pl.BlockSpec((CHUNK, ATTN_KV), lambda i: (i, kcol)),
        pl.BlockSpec((CHUNK, ATTN_KV), lambda i: (lat(i, 1), kcol)),
        pl.BlockSpec((CHUNK, ATTN_KV), lambda i: (lat(i, -1), vcol)),
        pl.BlockSpec((CHUNK, ATTN_KV), lambda i: (i, vcol)),
        pl.BlockSpec((CHUNK, ATTN_KV), lambda i: (lat(i, 1), vcol)),
    ]
    return pl.pallas_call(
        functools.partial(_attn_body, layer=layer, n_ctx_chunks=ncc, n_chunks=n_chunks, n_ctx=n_ctx),
        grid=(n_chunks,),
        in_specs=in_specs,
        out_specs=pl.BlockSpec((CHUNK, ATTN_Q), lambda i: (i, 0)),
        out_shape=jax.ShapeDtypeStruct((t, ATTN_Q), BF16),
        compiler_params=_cparams(("parallel",)),
        name="attention",
    )(sink, proj_r, proj_r, proj_p, proj_r, proj_r, proj_r, proj_p, proj_p, proj_p)


def _merge_body(ya_ref, yb_ref, yc_ref, gd_ref, wa_ref, wb_ref, wc_ref, ga_ref, gb_ref, gc_ref, *rest):
    n_jobs = (len(rest) - 1) // 2
    o_ref = rest[n_jobs]
    _run_cast_jobs(rest[:n_jobs], rest[n_jobs + 1:])
    gd = gd_ref[...]

    def branch(y_ref, w_ref, g_ref):
        gate = jax.nn.sigmoid(jnp.dot(gd, g_ref[...], preferred_element_type=F32))
        return gate * jnp.dot(y_ref[...], w_ref[...], preferred_element_type=F32)

    o_ref[...] = (branch(ya_ref, wa_ref, ga_ref) + branch(yb_ref, wb_ref, gb_ref)
                  + branch(yc_ref, wc_ref, gc_ref)).astype(o_ref.dtype)


def _weight_spec(w, layer, rows, tn, col_block):
    if w.ndim == 3:
        return pl.BlockSpec((None, rows, tn), lambda i, j: (layer, 0, col_block(j)))
    return pl.BlockSpec((rows, tn), lambda i, j: (0, col_block(j)))


def _merge(ya, yb, yc, proj_p, wa, wb, wc, wg, layer, moe_w2, *, tm, tn=512):
    t = ya.shape[0]
    d = wa.shape[-1]
    ni, nj = t // tm, d // tn
    act = lambda w: pl.BlockSpec((tm, w), lambda i, j: (i, 0))
    wsp = lambda w: _weight_spec(w, layer, w.shape[-2], tn, lambda j: j)
    gsp = lambda b: _weight_spec(wg, layer, GATE_RANK, tn, lambda j: b * nj + j)
    jobs = [_expert_cast_job(moe_w2, layer, ni * nj, nj, False)]
    outs = pl.pallas_call(
        _merge_body,
        grid=(ni, nj),
        in_specs=[act(SSD_D_INNER), act(RET_V), act(ATTN_Q),
                  pl.BlockSpec((tm, GATE_RANK), lambda i, j: (i, PL_GD // GATE_RANK)),
                  wsp(wa), wsp(wb), wsp(wc), gsp(0), gsp(1), gsp(2)] + [jb[1] for jb in jobs],
        out_specs=[pl.BlockSpec((tm, tn), lambda i, j: (i, j))] + [jb[3] for jb in jobs],
        out_shape=[jax.ShapeDtypeStruct((t, d), BF16)] + [jb[2] for jb in jobs],
        compiler_params=_cparams(("arbitrary", "arbitrary")),
        name="merge",
    )(ya, yb, yc, proj_p, wa, wb, wc, wg, wg, wg, *[jb[0] for jb in jobs])
    return outs[0], outs[1]


def _outproj_body(m_ref, w_ref, x_ref, mod_ref, o_ref, *, n_ctx, tm):
    acc = jnp.dot(m_ref[...], w_ref[...], preferred_element_type=F32)
    is_ctx = _is_ctx_rows(pl.program_id(0) * tm, tm, n_ctx)
    gate = jnp.where(is_ctx, mod_ref[0:1, :], mod_ref[1:2, :])
    o_ref[...] = x_ref[...] + gate * acc


def _outproj(merged, w, x, mods, layer, *, n_ctx, tm, tn=512):
    t, k = merged.shape
    d = w.shape[2]
    ni, nj = t // tm, d // tn
    k_gate = 2
    (out,) = pl.pallas_call(
        functools.partial(_outproj_body, n_ctx=n_ctx, tm=tm),
        grid=(ni, nj),
        in_specs=[
            pl.BlockSpec((tm, k), lambda i, j: (i, 0)),
            pl.BlockSpec((None, k, tn), lambda i, j: (layer, 0, j)),
            pl.BlockSpec((tm, tn), lambda i, j: (i, j)),
            pl.BlockSpec((None, 16, tn), lambda i, j: (layer, 0, k_gate * nj + j)),
        ],
        out_specs=[pl.BlockSpec((tm, tn), lambda i, j: (i, j))],
        out_shape=[jax.ShapeDtypeStruct((t, d), F32)],
        compiler_params=_cparams(("parallel", "arbitrary")),
        name="outproj",
    )(merged, w, x, mods)
    return out


def _router_logits(hb, wr_ref, rb_ref):
    return jnp.dot(hb, wr_ref[...], preferred_element_type=F32) + rb_ref[...]


def _router_body(x_ref, g_ref, mod_ref, wr_ref, rb_ref, gi_ref, *, n_ctx, tm):
    h = _norm_mod(x_ref[...], g_ref, mod_ref, 3, 4, pl.program_id(0) * tm, n_ctx)
    logits = _router_logits(h.astype(BF16), wr_ref, rb_ref)
    lane = lax.broadcasted_iota(I32, (tm, LANES), 1)
    is_g = (lane >= N_EXPERTS) & (lane < N_EXPERTS + N_GROUPS)
    gl = jnp.where(is_g, logits, -jnp.inf)
    gmax = jnp.max(gl, axis=-1, keepdims=True)
    first = jnp.min(jnp.where(gl == gmax, (lane - N_EXPERTS).astype(F32), float(N_GROUPS)), axis=-1, keepdims=True)
    gi_ref[...] = first.astype(I32)


def _router(x, g, mods, wr, rb, layer, *, n_ctx, tm=ROW_TM):
    t, d = x.shape
    return pl.pallas_call(
        functools.partial(_router_body, n_ctx=n_ctx, tm=tm),
        grid=(t // tm,),
        in_specs=[
            pl.BlockSpec((tm, d), lambda i: (i, 0)),
            _layer_vec(layer, d), _layer_mod(layer, d),
            pl.BlockSpec((None, d, LANES), lambda i: (layer, 0, 0)),
            _layer_vec(layer, LANES),
        ],
        out_specs=pl.BlockSpec((tm, 1), lambda i: (i, 0)),
        out_shape=jax.ShapeDtypeStruct((t, 1), I32),
        compiler_params=_cparams(("parallel",)),
        name="moe_router",
    )(x, g, mods, wr, rb)


def _positions_body(g_ref, pos_ref, tg_ref, *, n_rows, tm):
    gidx = g_ref[...]
    li = lax.broadcasted_iota(I32, (LANES, LANES), 0)
    lj = lax.broadcasted_iota(I32, (LANES, LANES), 1)
    upper = jnp.where(li <= lj, 1.0, 0.0).astype(BF16)
    ri = lax.broadcasted_iota(I32, (n_rows, n_rows), 0)
    rj = lax.broadcasted_iota(I32, (n_rows, n_rows), 1)
    strict = jnp.where(rj < ri, 1.0, 0.0).astype(BF16)
    tile_start = (lax.broadcasted_iota(I32, (8, LANES), 1) * tm).astype(F32)
    pos = jnp.zeros((n_rows, LANES), F32)
    tg = jnp.zeros((8, LANES), F32)
    off = jnp.zeros((1, 1), F32)
    for g in range(N_GROUPS):
        mk = jnp.where(gidx == g, 1.0, 0.0)
        within = jnp.dot(mk.astype(BF16), upper, preferred_element_type=F32)
        rowtot = jnp.broadcast_to(within[:, LANES - 1:LANES], (n_rows, LANES))
        before = jnp.dot(strict, rowtot.astype(BF16), preferred_element_type=F32)
        pos = pos + mk * (off + before + within - 1.0)
        count = jnp.sum(jnp.sum(mk, axis=-1, keepdims=True), axis=0, keepdims=True)
        padded = jnp.floor((count + (tm - 1.0)) * (1.0 / tm)) * tm
        off = off + padded
        tg = tg + jnp.where(tile_start >= off, 1.0, 0.0)
    pos_ref[...] = pos.astype(I32)
    lane = lax.broadcasted_iota(I32, (8, LANES), 1)
    n_used = off * (1.0 / tm)
    tgi = jnp.minimum(tg, N_GROUPS - 1.0)
    tg_ref[...] = jnp.where(lane == LANES - 1, n_used, tgi).astype(I32)


def _positions(gidx2d, *, tm):
    n_rows = gidx2d.shape[0]
    return pl.pallas_call(
        functools.partial(_positions_body, n_rows=n_rows, tm=tm),
        out_shape=[jax.ShapeDtypeStruct((n_rows, LANES), I32), jax.ShapeDtypeStruct((8, LANES), I32)],
        name="moe_positions",
    )(gidx2d)


ROW_DMA_UNROLL = 8


def _start_then_wait_rows(row_copy, n_rows):
    def start(r, c):
        row_copy(r).start()
        return c

    def wait(r, c):
        row_copy(r).wait()
        return c

    lax.fori_loop(0, n_rows, start, 0, unroll=ROW_DMA_UNROLL)
    lax.fori_loop(0, n_rows, wait, 0, unroll=ROW_DMA_UNROLL)


def _scatter_body(pos_ref, x_ref, g_ref, mod_ref, init_ref, o_ref, h_ref, sem, *, tm, n_ctx):
    del init_ref
    base = pl.program_id(0) * tm
    h_ref[...] = _norm_mod(x_ref[...], g_ref, mod_ref, 3, 4, base, n_ctx)

    def row_copy(r):
        return pltpu.make_async_copy(h_ref.at[pl.ds(r, 1), :], o_ref.at[pl.ds(pos_ref[base + r], 1), :], sem)

    _start_then_wait_rows(row_copy, tm)


def _scatter_rows(pos, x, g, mods, init, layer, *, n_ctx, tm=ROW_TM):
    t, d = x.shape
    return pl.pallas_call(
        functools.partial(_scatter_body, tm=tm, n_ctx=n_ctx),
        grid_spec=pltpu.PrefetchScalarGridSpec(
            num_scalar_prefetch=1,
            grid=(t // tm,),
            in_specs=[pl.BlockSpec((tm, d), lambda i, p: (i, 0)), _layer_vec(layer, d), _layer_mod(layer, d),
                      pl.BlockSpec(memory_space=pl.ANY)],
            out_specs=pl.BlockSpec(memory_space=pl.ANY),
            scratch_shapes=[pltpu.VMEM((tm, d), F32), pltpu.SemaphoreType.DMA(())],
        ),
        out_shape=jax.ShapeDtypeStruct(init.shape, init.dtype),
        input_output_aliases={4: 0},
        compiler_params=_cparams(("arbitrary",)),
        name="moe_scatter",
    )(pos, x, g, mods, init)


def _experts_body(tg_ref, x_ref, wr_ref, rb_ref, w1_ref, w3_ref, w2_ref, o_ref, xb_ref, gate_ref, *, tm):
    ti = pl.program_id(0)
    q = pl.program_id(1)
    grp = tg_ref[ti]
    used = ti < tg_ref[LANES - 1]
    lane = lax.broadcasted_iota(I32, (tm, LANES), 1)
    lane_f = lane.astype(F32)

    @pl.when(jnp.logical_not(used) & (q == 0))
    def _():
        o_ref[...] = jnp.zeros_like(o_ref)

    @pl.when(used & (q == 0))
    def _():
        xb = x_ref[...].astype(BF16)
        xb_ref[...] = xb
        logits = _router_logits(xb, wr_ref, rb_ref)
        is_g = (lane >= N_EXPERTS) & (lane < N_EXPERTS + N_GROUPS)
        gl = jnp.where(is_g, logits, -jnp.inf)
        gmax = jnp.max(gl, axis=-1, keepdims=True)
        gsum = jnp.sum(jnp.exp(gl - gmax), axis=-1, keepdims=True)
        gsel = jnp.sum(jnp.where(lane == N_EXPERTS + grp, logits, 0.0), axis=-1, keepdims=True)
        p_grp = jnp.exp(gsel - gmax) / gsum
        in_grp = (lane >= grp * EXPERTS_PER_GROUP) & (lane < (grp + 1) * EXPERTS_PER_GROUP)
        el = jnp.where(in_grp, logits, -jnp.inf)
        v1 = jnp.max(el, axis=-1, keepdims=True)
        i1 = jnp.min(jnp.where(el == v1, lane_f, float(LANES)), axis=-1, keepdims=True)
        el2 = jnp.where(lane_f == i1, -jnp.inf, el)
        v2 = jnp.max(el2, axis=-1, keepdims=True)
        i2 = jnp.min(jnp.where(el2 == v2, lane_f, float(LANES)), axis=-1, keepdims=True)
        e2 = jnp.exp(v2 - v1)
        w1 = p_grp / (1.0 + e2)
        w2 = p_grp * e2 / (1.0 + e2)
        gate_ref[...] = jnp.where(lane_f == i1, w1, 0.0) + jnp.where(lane_f == i2, w2, 0.0)

    @pl.when(used)
    def _():
        xb = xb_ref[...]
        a = jnp.dot(xb, w1_ref[...], preferred_element_type=F32)
        u = jnp.dot(xb, w3_ref[...], preferred_element_type=F32)
        gates = gate_ref[...]
        per = MOE_SLAB // EXPERT_FF
        cols = []
        for j in range(per):
            e = grp * EXPERTS_PER_GROUP + q * per + j
            ge = jnp.sum(jnp.where(lane == e, gates, 0.0), axis=-1, keepdims=True)
            cols.append(jnp.broadcast_to(ge, (tm, EXPERT_FF)))
        hid = (_silu(a) * u * jnp.concatenate(cols, axis=1)).astype(BF16)
        part = jnp.dot(hid, w2_ref[...], preferred_element_type=F32)

        @pl.when(q == 0)
        def _():
            o_ref[...] = part

        @pl.when(q > 0)
        def _():
            o_ref[...] += part


def _experts(tile_group, xs, wr, rb, w1, w3, w2, layer, *, tm):
    n_rows, d = xs.shape
    n_tiles = n_rows // tm
    return pl.pallas_call(
        functools.partial(_experts_body, tm=tm),
        grid_spec=pltpu.PrefetchScalarGridSpec(
            num_scalar_prefetch=1,
            grid=(n_tiles, MOE_SPLIT),
            in_specs=[
                pl.BlockSpec((tm, d), lambda i, q, tg: (i, 0)),
                pl.BlockSpec((None, d, LANES), lambda i, q, tg: (layer, 0, 0)),
                pl.BlockSpec((None, 1, LANES), lambda i, q, tg: (layer, 0, 0)),
                pl.BlockSpec((d, MOE_SLAB), lambda i, q, tg: (0, tg[i] * MOE_SPLIT + q)),
                pl.BlockSpec((d, MOE_SLAB), lambda i, q, tg: (0, tg[i] * MOE_SPLIT + q)),
                pl.BlockSpec((MOE_SLAB, d), lambda i, q, tg: (tg[i] * MOE_SPLIT + q, 0)),
            ],
            out_specs=pl.BlockSpec((tm, d), lambda i, q, tg: (i, 0)),
            scratch_shapes=[pltpu.VMEM((tm, d), BF16), pltpu.VMEM((tm, LANES), F32)],
        ),
        out_shape=jax.ShapeDtypeStruct((n_rows, d), F32),
        compiler_params=_cparams(("arbitrary", "arbitrary")),
        name="moe_experts",
    )(tile_group, xs, wr, rb, w1, w3, w2)


def _combine_body(pos_ref, x_ref, mod_ref, ys_ref, g_ref, modn_ref, *rest, tm, n_ctx, first_row, final):
    if final:
        o_ref, buf_ref, sem = rest
    else:
        o_ref, h_ref, buf_ref, sem = rest
    base = first_row + pl.program_id(0) * tm

    def row_copy(r):
        return pltpu.make_async_copy(ys_ref.at[pl.ds(pos_ref[base + r], 1), :], buf_ref.at[pl.ds(r, 1), :], sem)

    _start_then_wait_rows(row_copy, tm)
    x = x_ref[...] + _mod_row(mod_ref, 5, base, n_ctx) * buf_ref[...]
    if final:
        o_ref[...] = x * lax.rsqrt(jnp.mean(x * x, axis=-1, keepdims=True) + NORM_EPS) * g_ref[...]
    else:
        o_ref[...] = x
        h_ref[...] = _norm_mod(x, g_ref, modn_ref, 0, 1, base, n_ctx).astype(h_ref.dtype)


def _combine(pos, x, mods, ys, g_next, layer, *, n_ctx, final, tm=ROW_TM):
    t, d = x.shape
    first_tile = n_ctx // tm if final else 0
    n_out = t - first_tile * tm
    next_layer = layer if final else layer + 1
    g_spec = pl.BlockSpec((1, d), lambda i, p: (0, 0)) if final else _layer_vec(next_layer, d)
    out_specs = [pl.BlockSpec((tm, d), lambda i, p: (i, 0))]
    out_shape = [jax.ShapeDtypeStruct((n_out, d), F32)]
    if not final:
        out_specs.append(pl.BlockSpec((tm, d), lambda i, p: (i, 0)))
        out_shape.append(jax.ShapeDtypeStruct((t, d), BF16))
    return pl.pallas_call(
        functools.partial(_combine_body, tm=tm, n_ctx=n_ctx, first_row=first_tile * tm, final=final),
        grid_spec=pltpu.PrefetchScalarGridSpec(
            num_scalar_prefetch=1,
            grid=(n_out // tm,),
            in_specs=[
                pl.BlockSpec((tm, d), lambda i, p: (i + first_tile, 0)),
                _layer_mod(layer, d),
                pl.BlockSpec(memory_space=pl.ANY),
                g_spec,
                _layer_mod(next_layer, d),
            ],
            out_specs=out_specs,
            scratch_shapes=[pltpu.VMEM((tm, d), F32), pltpu.SemaphoreType.DMA(())],
        ),
        out_shape=out_shape,
        compiler_params=_cparams(("arbitrary",)),
        name="moe_combine_final" if final else "moe_combine",
    )(pos, x, mods, ys, g_next, mods)


_W_IN_OFF = tuple(sum(IN_SPLITS[:k]) for k in range(len(IN_SPLITS)))
_SEG_Z, _SEG_XBC, _SEG_DT, _SEG_RQ, _SEG_RK, _SEG_RV, _SEG_RG, _SEG_AQ, _SEG_AK, _SEG_AV, _SEG_GD = range(len(IN_SPLITS))
_ROPE_SRC = ((_SEG_AQ, 0, ATTN_Q), (_SEG_RQ, 0, RET_Q), (_SEG_RK, 0, RET_Q), (_SEG_AK, 0, ATTN_KV))
_PLAIN_SRC = ((_SEG_Z, 0, SSD_D_INNER), (_SEG_XBC, 0, SSD_D_INNER), (_SEG_RV, 0, RET_V), (_SEG_RG, 0, RET_V),
              (_SEG_AV, 0, ATTN_KV), (_SEG_XBC, SSD_D_INNER, SSD_BC), (_SEG_XBC, SSD_D_INNER + SSD_BC, SSD_BC),
              (_SEG_GD, 0, GATE_RANK))
W_IN_PREP_ROWS = 64


def _prep_w_in_body(w_ref, rope_ref, plain_ref, dt_ref):
    def copy_segments(dst, segments):
        o = 0
        for seg, inner, width in segments:
            src = _W_IN_OFF[seg] + inner
            dst[:, o:o + width] = w_ref[:, src:src + width].astype(dst.dtype)
            o += width

    copy_segments(rope_ref, _ROPE_SRC)
    copy_segments(plain_ref, _PLAIN_SRC)
    dt0 = _W_IN_OFF[_SEG_DT]
    dt_ref[...] = jnp.zeros_like(dt_ref)
    dt_ref[:, 0:SSD_HEADS] = w_ref[:, dt0:dt0 + SSD_HEADS].astype(dt_ref.dtype)
    dt_ref[:, LANES:LANES + SSD_HEADS] = w_ref[:, dt0 + SSD_HEADS:dt0 + 2 * SSD_HEADS].astype(dt_ref.dtype)


def _prep_w_in(w_in):
    depth, k, n = w_in.shape
    r = W_IN_PREP_ROWS
    blk = lambda w: pl.BlockSpec((None, r, w), lambda l, i: (l, i, 0))
    return pl.pallas_call(
        _prep_w_in_body,
        grid=(depth, k // r),
        in_specs=[blk(n)],
        out_specs=[blk(ROPE_W), blk(PLAIN_W), blk(DT_W)],
        out_shape=[jax.ShapeDtypeStruct((depth, k, w), BF16) for w in (ROPE_W, PLAIN_W, DT_W)],
        compiler_params=_cparams(("parallel", "arbitrary")),
        name="prep_w_in",
    )(w_in)


def _rope_tables(n_ctx, seq):
    n_rows = seq // GRID_W
    row = jnp.repeat(jnp.arange(n_rows), GRID_W).astype(F32)
    col = jnp.tile(jnp.arange(GRID_W), n_rows).astype(F32)
    n_freq = ROPE_DIM // 4
    inv = ROPE_BASE ** (-jnp.arange(n_freq, dtype=F32) / n_freq)
    ang = jnp.concatenate([row[:, None] * inv, row[:, None] * inv, col[:, None] * inv, col[:, None] * inv], axis=-1)
    sign = jnp.where((jnp.arange(ROPE_DIM) % (ROPE_DIM // 2)) < ROPE_DIM // 4, -1.0, 1.0).astype(F32)
    cos = jnp.concatenate([jnp.ones((n_ctx, ROPE_DIM), F32), jnp.cos(ang)], axis=0)
    sin = jnp.concatenate([jnp.zeros((n_ctx, ROPE_DIM), F32), jnp.sin(ang) * sign], axis=0)
    return cos, sin


def _row_tile(t):
    for tm in (1056, 1024, 768, 512, 384, 256):
        if t % tm == 0:
            return tm
    raise ValueError(f"token count {t} is not a multiple of 256")


def kernel(x, c, ctx, c_ctx, norm1_g, norm2_g, mod_down, mod_up, mod_b, w_in, conv_w, conv_b, ssd_a_log, ssd_dt_bias, ssd_d, ssd_norm_g, ret_log_decay, attn_sink, w_branch_a, w_branch_b, w_branch_c, w_gate_up, w_out, router_group_w, router_group_b, router_expert_w, router_expert_b, moe_w1, moe_w3, moe_w2, final_norm_g):
    assert x.shape[0] == 1 and ctx.shape[0] == 1
    seq, d = x.shape[1], x.shape[2]
    n_ctx = ctx.shape[1]
    depth = w_in.shape[0]
    assert d == D_MODEL and seq % ROW_TM == 0 and n_ctx % ROW_TM == 0 and seq % GRID_W == 0
    t = n_ctx + seq
    tm = _row_tile(t)
    tm_merge = 768 if t % 768 == 0 else ROW_TM

    w_rope, w_plain, w_dt = _prep_w_in(w_in)
    wb = w_branch_b.astype(BF16)
    wc = w_branch_c.astype(BF16)
    wo = w_out.astype(BF16)
    wr = jnp.concatenate([router_expert_w, router_group_w,
                          jnp.zeros((depth, d, LANES - N_EXPERTS - N_GROUPS), F32)], axis=-1).astype(BF16)
    rb = jnp.concatenate([router_expert_b, router_group_b,
                          jnp.zeros((depth, LANES - N_EXPERTS - N_GROUPS), F32)], axis=-1).reshape(depth, 1, LANES)
    conv_w8 = jnp.concatenate([conv_w, jnp.zeros((depth, 8 - SSD_CONV, SSD_XBC), F32)], axis=1)
    conv_b3 = conv_b.reshape(depth, 1, SSD_XBC)
    hpad = jnp.zeros((depth, 2, LANES - SSD_HEADS), F32)
    dtb = jnp.concatenate([ssd_dt_bias, hpad], axis=-1).reshape(depth, 2, 1, LANES)
    alog = jnp.concatenate([ssd_a_log, hpad], axis=-1).reshape(depth, 2, 1, LANES)
    d_x = jnp.repeat(ssd_d, SSD_HEAD_DIM, axis=-1).reshape(depth, 1, SSD_D_INNER)
    ssd_g = ssd_norm_g.reshape(depth, 1, SSD_D_INNER)
    g1 = norm1_g.reshape(depth, 1, d)
    g2 = norm2_g.reshape(depth, 1, d)

    xres = jnp.concatenate([ctx[0], x[0]], axis=0)
    cc = jnp.zeros((16, d), F32).at[0].set(c_ctx).at[1].set(c[0])
    mods = _modulation(cc, mod_down, mod_up, mod_b)
    cos, sin = _rope_tables(n_ctx, seq)
    colscale = jnp.ones((1, ROPE_W), F32).at[:, ROPE_RK:ROPE_RK + RET_Q].set(RET_QK_DIM ** -0.5)
    e64 = (jnp.arange(LANES)[:, None] == (jnp.arange(SSD_D_INNER)[None, :] // SSD_HEAD_DIM)).astype(BF16)
    n_idx_rows = -(-(t // LANES) // 8) * 8
    n_sorted = (t // MOE_TM + N_GROUPS) * MOE_TM
    sorted_buf = jnp.zeros((n_sorted, d), F32)

    h = _normmod(xres, g1, mods, 0, n_ctx=n_ctx)
    out = None
    for i in range(depth):
        proj_r, (wa, wg) = _proj_rope(h, w_rope, i, cos, sin, colscale,
                                      [(w_branch_a, 0, 32), (w_gate_up, 1, 8)], tm=tm)
        nj_plain = PLAIN_W // PROJ_TN
        plain_jobs = [_expert_cast_job(moe_w1, i, (t // tm) * nj_plain, nj_plain, True),
                      _expert_cast_job(moe_w3, i, (t // tm) * nj_plain, nj_plain, True)]
        proj_p, (w1, w3) = _matmul(h, w_plain, i, plain_jobs, tm=tm, tn=PROJ_TN, out_dtype=BF16, name="proj_plain")
        dtraw, _ = _matmul(h, w_dt, i, [], tm=tm, tn=DT_W, out_dtype=F32, name="proj_dt")

        yf, xs_c, b_c, c_c = _ssd_forward(proj_p, dtraw, conv_w8, conv_b3, dtb, alog, e64, i, n_ctx=n_ctx)
        ya = _ssd_backward(proj_p, xs_c, b_c, c_c, dtraw, dtb, alog, e64, yf, d_x, ssd_g, i, n_ctx=n_ctx)
        rf = _retention(proj_r, proj_p, ret_log_decay, None, i, backward=False, n_ctx=n_ctx)
        yb = _retention(proj_r, proj_p, ret_log_decay, rf, i, backward=True, n_ctx=n_ctx)
        yc = _attention(proj_r, proj_p, attn_sink, i, n_ctx=n_ctx)

        merged, w2 = _merge(ya, yb, yc, proj_p, wa, wb, wc, wg, i, moe_w2, tm=tm_merge)
        xres = _outproj(merged, wo, xres, mods, i, n_ctx=n_ctx, tm=tm)

        gidx = _router(xres, g2, mods, wr, rb, i, n_ctx=n_ctx)
        gidx2d = jnp.concatenate([gidx[:, 0], jnp.full((n_idx_rows * LANES - t,), N_GROUPS, I32)]).reshape(n_idx_rows, LANES)
        pos2d, tile_group = _positions(gidx2d, tm=MOE_TM)
        pos = pos2d.reshape(-1)[:t]
        sorted_buf = _scatter_rows(pos, xres, g2, mods, sorted_buf, i, n_ctx=n_ctx)
        ys = _experts(tile_group[0], sorted_buf, wr, rb, w1, w3, w2, i, tm=MOE_TM)
        if i + 1 < depth:
            xres, h = _combine(pos, xres, mods, ys, g1, i, n_ctx=n_ctx, final=False)
        else:
            (out,) = _combine(pos, xres, mods, ys, final_norm_g.reshape(1, d), i, n_ctx=n_ctx, final=True)
    return out[None]
```

```python
import functools
import math

import jax
import jax.numpy as jnp
from jax import lax
from jax.experimental import pallas as pl
from jax.experimental.pallas import tpu as pltpu

F32 = jnp.float32
BF16 = jnp.bfloat16
I32 = jnp.int32

D_MODEL = 4096
GRID_W = 64
CHUNK = 128
NORM_EPS = 1e-6
NEG_INF = -1e30
MOD_RANK = D_MODEL // 16
SSD_D_INNER = D_MODEL // 2
SSD_HEAD_DIM = 64
SSD_HEADS = SSD_D_INNER // SSD_HEAD_DIM
SSD_GROUPS = 4
SSD_HPG = SSD_HEADS // SSD_GROUPS
SSD_STATE = 128
SSD_CONV = 5
SSD_BC = SSD_GROUPS * SSD_STATE
RET_HEADS = 8
RET_QK_DIM = 128
RET_V_DIM = 2 * RET_QK_DIM
ATTN_HEADS = 16
ATTN_KV_HEADS = 4
ATTN_HEAD_DIM = 128
ATTN_GRP = ATTN_HEADS // ATTN_KV_HEADS
ROPE_DIM = 128
ROPE_BASE = 10000.0
GATE_RANK = D_MODEL // 16
N_GROUPS = 8
EXPERTS_PER_GROUP = 8
N_EXPERTS = N_GROUPS * EXPERTS_PER_GROUP
EXPERT_FF = D_MODEL // 32
SSD_XBC = SSD_D_INNER + 2 * SSD_BC
RET_Q = RET_HEADS * RET_QK_DIM
RET_V = RET_HEADS * RET_V_DIM
ATTN_Q = ATTN_HEADS * ATTN_HEAD_DIM
ATTN_KV = ATTN_KV_HEADS * ATTN_HEAD_DIM
IN_SPLITS = (SSD_D_INNER, SSD_XBC, 2 * SSD_HEADS, RET_Q, RET_Q, RET_V, RET_V, ATTN_Q, ATTN_KV, ATTN_KV, GATE_RANK)

LANES = 128
BF16_SUBLANES = 16
VMEM_LIMIT = 56 * 1024 * 1024

ROPE_AQ = 0
ROPE_RQ = ROPE_AQ + ATTN_Q
ROPE_RK = ROPE_RQ + RET_Q
ROPE_AK = ROPE_RK + RET_Q
ROPE_W = ROPE_AK + ATTN_KV
PL_Z = 0
PL_XS = PL_Z + SSD_D_INNER
PL_RV = PL_XS + SSD_D_INNER
PL_RG = PL_RV + RET_V
PL_AV = PL_RG + RET_V
PL_B = PL_AV + ATTN_KV
PL_C = PL_B + SSD_BC
PL_GD = PL_C + SSD_BC
PLAIN_W = PL_GD + GATE_RANK
PROJ_TN = 768
DT_W = 2 * LANES

MOE_TM = 256
MOE_SPLIT = 2
MOE_SLAB = EXPERTS_PER_GROUP * EXPERT_FF // MOE_SPLIT
ROW_TM = 256


def _cparams(sem):
    return pltpu.CompilerParams(dimension_semantics=sem, vmem_limit_bytes=VMEM_LIMIT)


def _silu(x):
    return x * (1.0 / (1.0 + jnp.exp(-x)))


def _split3(x):
    hi = x.astype(BF16)
    r1 = x - hi.astype(F32)
    mid = r1.astype(BF16)
    lo = (r1 - mid.astype(F32)).astype(BF16)
    return hi, mid, lo


def _dot01_left(m01, x):
    hi, mid, lo = _split3(x)
    d = lambda t: jnp.dot(m01, t, preferred_element_type=F32)
    return d(hi) + d(mid) + d(lo)


def _dot01_right(x, m01):
    hi, mid, lo = _split3(x)
    d = lambda t: jnp.dot(t, m01, preferred_element_type=F32)
    return d(hi) + d(mid) + d(lo)


def _mod_body(c_ref, down_ref, up_ref, b_ref, o_ref):
    s = _silu(c_ref[...])
    r = jnp.dot(s.astype(BF16), down_ref[...].astype(BF16), preferred_element_type=F32)
    o_ref[...] = jnp.dot(r.astype(BF16), up_ref[...].astype(BF16), preferred_element_type=F32) + b_ref[...]


def _modulation(cc, mod_down, mod_up, mod_b):
    depth = mod_down.shape[0]
    d = cc.shape[1]
    return pl.pallas_call(
        _mod_body,
        grid=(depth, 6),
        in_specs=[
            pl.BlockSpec((16, d), lambda l, j: (0, 0)),
            pl.BlockSpec((None, d, MOD_RANK), lambda l, j: (l, 0, 0)),
            pl.BlockSpec((None, MOD_RANK, d), lambda l, j: (l, 0, j)),
            pl.BlockSpec((None, 1, d), lambda l, j: (l, 0, j)),
        ],
        out_specs=pl.BlockSpec((None, 16, d), lambda l, j: (l, 0, j)),
        out_shape=jax.ShapeDtypeStruct((depth, 16, 6 * d), F32),
        compiler_params=_cparams(("arbitrary", "arbitrary")),
        name="modulation",
    )(cc, mod_down, mod_up, mod_b.reshape(depth, 1, 6 * d))


def _mod_row(mod_ref, k, first_row, n_ctx):
    d = D_MODEL
    r = jnp.where(first_row < n_ctx, 0, 1)
    return mod_ref[pl.ds(r, 1), k * d:(k + 1) * d]


def _norm_mod(x, g_ref, mod_ref, k_shift, k_scale, first_row, n_ctx):
    rstd = lax.rsqrt(jnp.mean(x * x, axis=-1, keepdims=True) + NORM_EPS)
    gain = g_ref[...] * (1.0 + _mod_row(mod_ref, k_scale, first_row, n_ctx))
    return x * rstd * gain + _mod_row(mod_ref, k_shift, first_row, n_ctx)


def _is_ctx_rows(first_row, tm, n_ctx):
    return (first_row + lax.broadcasted_iota(I32, (tm, 1), 0)) < n_ctx


def _layer_vec(layer, d):
    return pl.BlockSpec((None, 1, d), lambda *_: (layer, 0, 0))


def _layer_mod(layer, d):
    return pl.BlockSpec((None, 16, 6 * d), lambda *_: (layer, 0, 0))


def _normmod_body(x_ref, g_ref, mod_ref, o_ref, *, n_ctx, tm):
    o_ref[...] = _norm_mod(x_ref[...], g_ref, mod_ref, 0, 1, pl.program_id(0) * tm, n_ctx).astype(o_ref.dtype)


def _normmod(x, g, mods, layer, *, n_ctx, tm=ROW_TM):
    t, d = x.shape
    return pl.pallas_call(
        functools.partial(_normmod_body, n_ctx=n_ctx, tm=tm),
        grid=(t // tm,),
        in_specs=[pl.BlockSpec((tm, d), lambda i: (i, 0)), _layer_vec(layer, d), _layer_mod(layer, d)],
        out_specs=pl.BlockSpec((tm, d), lambda i: (i, 0)),
        out_shape=jax.ShapeDtypeStruct((t, d), BF16),
        compiler_params=_cparams(("parallel",)),
        name="normmod",
    )(x, g, mods)


def _pow2_floor(n):
    return 1 << (max(int(n), 1).bit_length() - 1)


def _cast_job(w, layer, axis, n_blocks, start, steps_per_block, nj):
    _, a, b = w.shape
    rb, cb = (a // n_blocks, b) if axis == 0 else (a, b // n_blocks)

    def blk(i, j):
        return jnp.clip((i * nj + j - start) // steps_per_block, 0, n_blocks - 1)

    if axis == 0:
        in_spec = pl.BlockSpec((None, rb, cb), lambda i, j: (layer, blk(i, j), 0))
        out_spec = pl.BlockSpec((rb, cb), lambda i, j: (blk(i, j), 0))
    else:
        in_spec = pl.BlockSpec((None, rb, cb), lambda i, j: (layer, 0, blk(i, j)))
        out_spec = pl.BlockSpec((rb, cb), lambda i, j: (0, blk(i, j)))
    return w, in_spec, jax.ShapeDtypeStruct((a, b), BF16), out_spec


def _expert_cast_job(w, layer, n_steps, nj, side_by_side):
    _, e, r, c = w.shape
    n_blocks = _pow2_floor(n_steps)
    epb = max(1, e // n_blocks)
    parts = max(1, n_blocks // e)
    n_blocks = (e // epb) * parts
    steps_per_block = n_steps // n_blocks

    def blk(i, j):
        return jnp.clip((i * nj + j) // steps_per_block, 0, n_blocks - 1)

    if side_by_side:
        rp = r // parts
        in_spec = pl.BlockSpec((None, epb, rp, c), lambda i, j: (layer, blk(i, j) // parts, blk(i, j) % parts, 0))
        out_spec = pl.BlockSpec((rp, epb * c), lambda i, j: (blk(i, j) % parts, blk(i, j) // parts))
        return w, in_spec, jax.ShapeDtypeStruct((r, e * c), BF16), out_spec
    cp = c // parts
    in_spec = pl.BlockSpec((None, epb, r, cp), lambda i, j: (layer, blk(i, j) // parts, 0, blk(i, j) % parts))
    out_spec = pl.BlockSpec((epb * r, cp), lambda i, j: (blk(i, j) // parts, blk(i, j) % parts))
    return w, in_spec, jax.ShapeDtypeStruct((e * r, c), BF16), out_spec


def _run_cast_jobs(src_refs, dst_refs):
    for src, dst in zip(src_refs, dst_refs):
        if len(src.shape) == len(dst.shape):
            dst[...] = src[...].astype(dst.dtype)
            continue
        e, r, c = src.shape
        for k in range(e):
            if dst.shape[0] == r:
                dst[:, k * c:(k + 1) * c] = src[k].astype(dst.dtype)
            else:
                dst[k * r:(k + 1) * r, :] = src[k].astype(dst.dtype)


def _dot_nt(x, w_t):
    return lax.dot_general(x, w_t, (((1,), (1,)), ((), ())), preferred_element_type=F32)


def _proj_rope_body(x_ref, w_ref, cos_ref, sin_ref, cs_ref, *rest, tm, tn):
    n_jobs = (len(rest) - 1) // 2
    o_ref = rest[n_jobs]
    _run_cast_jobs(rest[:n_jobs], rest[n_jobs + 1:])
    acc = _dot_nt(x_ref[...], w_ref[...])
    cos = cos_ref[...]
    sin = sin_ref[...]
    lane = lax.broadcasted_iota(I32, (tm, LANES), 1)
    first_half = (lane % (ROPE_DIM // 2)) < (ROPE_DIM // 4)
    for h in range(tn // LANES):
        sl = slice(h * LANES, (h + 1) * LANES)
        a = acc[:, sl]
        rot = jnp.where(first_half, pltpu.roll(a, ROPE_DIM - ROPE_DIM // 4, 1), pltpu.roll(a, ROPE_DIM // 4, 1))
        o_ref[:, sl] = ((a * cos + rot * sin) * cs_ref[:, sl]).astype(o_ref.dtype)


def _mm_body(x_ref, w_ref, *rest):
    n_jobs = (len(rest) - 1) // 2
    o_ref = rest[n_jobs]
    _run_cast_jobs(rest[:n_jobs], rest[n_jobs + 1:])
    o_ref[...] = _dot_nt(x_ref[...], w_ref[...]).astype(o_ref.dtype)


def _sequential_cast_jobs(layer, n_steps, nj, specs):
    fits = sum(s[2] for s in specs) <= n_steps
    share = _pow2_floor(n_steps // len(specs))
    jobs, start = [], 0
    for w, axis, max_blocks in specs:
        nb = max_blocks if fits else min(max_blocks, share)
        jobs.append(_cast_job(w, layer, axis, nb, start, 1, nj))
        start += nb
    return jobs


def _proj_rope(h, w, layer, cos, sin, colscale, cast_specs, *, tm):
    t, k = h.shape
    n = ROPE_W
    tn = PROJ_TN
    ni, nj = t // tm, n // tn
    jobs = _sequential_cast_jobs(layer, ni * nj, nj, cast_specs)
    outs = pl.pallas_call(
        functools.partial(_proj_rope_body, tm=tm, tn=tn),
        grid=(ni, nj),
        in_specs=[
            pl.BlockSpec((tm, k), lambda i, j: (i, 0)),
            pl.BlockSpec((None, tn, k), lambda i, j: (layer, j, 0)),
            pl.BlockSpec((tm, LANES), lambda i, j: (i, 0)),
            pl.BlockSpec((tm, LANES), lambda i, j: (i, 0)),
            pl.BlockSpec((1, tn), lambda i, j: (0, j)),
        ] + [jb[1] for jb in jobs],
        out_specs=[pl.BlockSpec((tm, tn), lambda i, j: (i, j))] + [jb[3] for jb in jobs],
        out_shape=[jax.ShapeDtypeStruct((t, n), BF16)] + [jb[2] for jb in jobs],
        compiler_params=_cparams(("arbitrary", "arbitrary")),
        name="proj_rope",
    )(h, w, cos, sin, colscale, *[jb[0] for jb in jobs])
    return outs[0], outs[1:]


def _matmul(x, w_t, layer, row0, n, jobs, *, tm, tn, out_dtype, name):
    t, k = x.shape
    blk0 = row0 // tn
    outs = pl.pallas_call(
        _mm_body,
        grid=(t // tm, n // tn),
        in_specs=[pl.BlockSpec((tm, k), lambda i, j: (i, 0)),
                  pl.BlockSpec((None, tn, k), lambda i, j: (layer, blk0 + j, 0))] + [jb[1] for jb in jobs],
        out_specs=[pl.BlockSpec((tm, tn), lambda i, j: (i, j))] + [jb[3] for jb in jobs],
        out_shape=[jax.ShapeDtypeStruct((t, n), out_dtype)] + [jb[2] for jb in jobs],
        compiler_params=_cparams(("arbitrary", "arbitrary")),
        name=name,
    )(x, w_t, *[jb[0] for jb in jobs])
    return outs[0], outs[1:]


def _chunk_of_step(i, *, backward, n_ctx_chunks, n_chunks):
    if not backward:
        return i
    return jnp.where(i < n_ctx_chunks, n_ctx_chunks - 1 - i, n_chunks - 1 - (i - n_ctx_chunks))


def _tri(backward):
    l = lax.broadcasted_iota(I32, (CHUNK, CHUNK), 0)
    s = lax.broadcasted_iota(I32, (CHUNK, CHUNK), 1)
    return (s >= l) if backward else (s <= l)


def _conv_silu(ext_ref, prev_ref, cur_ref, next_ref, w_ref, b_ref, zero_prev, zero_next):
    half = SSD_CONV // 2
    pad = 8
    hp = prev_ref[...].astype(F32)[BF16_SUBLANES - pad:, :]
    hn = next_ref[...].astype(F32)[:pad, :]
    ext_ref[0:pad, :] = hp * jnp.where(zero_prev, 0.0, 1.0)
    ext_ref[pad:pad + CHUNK, :] = cur_ref[...].astype(F32)
    ext_ref[pad + CHUNK:, :] = hn * jnp.where(zero_next, 0.0, 1.0)
    acc = None
    for j in range(SSD_CONV):
        term = ext_ref[pad - half + j:pad - half + j + CHUNK, :] * w_ref[j:j + 1, :]
        acc = term if acc is None else acc + term
    return _silu(acc + b_ref[...])


def _ssd_body(*refs, backward, finalize, n_ctx_chunks, n_chunks):
    if finalize:
        (xs_ref, b_ref, c_ref, dt_ref, dtb_ref, alog_ref, e64_ref, yf_ref, z_ref, dx_ref, ng_ref,
         y_ref, state_ref) = refs
    else:
        (xs_p, xs_c, xs_n, b_p, b_c, b_n, c_p, c_c, c_n, wx_ref, wb_ref, wc_ref, bx_ref, bb_ref, bc_ref,
         dt_ref, dtb_ref, alog_ref, e64_ref,
         y_ref, xs_o, b_o, c_o, state_ref, extx_ref, extb_ref) = refs
    i = pl.program_id(0)
    chunk = _chunk_of_step(i, backward=backward, n_ctx_chunks=n_ctx_chunks, n_chunks=n_chunks)

    @pl.when(i == 0)
    def _():
        state_ref[...] = jnp.zeros_like(state_ref)

    if finalize:
        xs = xs_ref[...].astype(F32)
        bm = b_ref[...]
        cm = c_ref[...]
    else:
        zero_prev = (chunk == 0) | (chunk == n_ctx_chunks)
        zero_next = (chunk == n_ctx_chunks - 1) | (chunk == n_chunks - 1)
        xs = _conv_silu(extx_ref, xs_p, xs_c, xs_n, wx_ref, bx_ref, zero_prev, zero_next)
        bm = _conv_silu(extb_ref, b_p, b_c, b_n, wb_ref, bb_ref, zero_prev, zero_next).astype(BF16)
        cm = _conv_silu(extb_ref, c_p, c_c, c_n, wc_ref, bc_ref, zero_prev, zero_next).astype(BF16)
        xs_o[...] = xs.astype(BF16)
        b_o[...] = bm
        c_o[...] = cm
    xs_b = xs.astype(BF16)

    x = dt_ref[...] + dtb_ref[...]
    dtp = jnp.maximum(x, 0.0) + jnp.log(1.0 + jnp.exp(-jnp.abs(x)))
    la = -jnp.exp(alog_ref[...]) * dtp
    mask = _tri(backward)
    tri01 = jnp.where(mask, 1.0, 0.0).astype(BF16)
    acs = _dot01_left(tri01, la) * math.log2(math.e)
    total = acs[0:1, :] if backward else acs[CHUNK - 1:CHUNK, :]
    src_t = (acs - jnp.log2(dtp)).T
    wd_t = (jnp.exp2(total - acs) * dtp).T
    cdec = jnp.broadcast_to(jnp.exp2(total), (8, LANES))
    cdec_x = _dot01_right(cdec, e64_ref[...])[0:1, :]
    lane = lax.broadcasted_iota(I32, (CHUNK, LANES), 1)
    low = lane < SSD_HEAD_DIM

    y_pairs = []
    for g in range(SSD_GROUPS):
        q_g = cm[:, g * SSD_STATE:(g + 1) * SSD_STATE]
        k_g = bm[:, g * SSD_STATE:(g + 1) * SSD_STATE]
        scores = lax.dot_general(q_g, k_g, (((1,), (1,)), ((), ())), preferred_element_type=F32)
        k_t = k_g.astype(F32).T
        for pr in range(SSD_HPG // 2):
            ha = g * SSD_HPG + 2 * pr
            hb = ha + 1
            pair = ha // 2
            sl = slice(pair * LANES, (pair + 1) * LANES)
            col_a = jnp.broadcast_to(acs[:, ha:ha + 1], (CHUNK, CHUNK))
            col_b = jnp.broadcast_to(acs[:, hb:hb + 1], (CHUNK, CHUNK))
            m_a = scores * jnp.exp2(jnp.where(mask, col_a - src_t[ha:ha + 1, :], -jnp.inf))
            m_b = scores * jnp.exp2(jnp.where(mask, col_b - src_t[hb:hb + 1, :], -jnp.inf))
            s_a = k_t * wd_t[ha:ha + 1, :]
            s_b = k_t * wd_t[hb:hb + 1, :]
            lhs = jnp.concatenate(
                [jnp.concatenate([m_a, m_b], axis=1), jnp.concatenate([s_a, s_b], axis=1)], axis=0).astype(BF16)
            xp = xs_b[:, sl]
            zero = jnp.zeros_like(xp)
            rhs = jnp.concatenate([jnp.where(low, xp, zero), jnp.where(low, zero, xp)], axis=0)
            both = jnp.dot(lhs, rhs, preferred_element_type=F32)
            prev = state_ref[pair]
            inter = jnp.dot(q_g, prev.astype(BF16), preferred_element_type=F32)
            inter = inter * jnp.exp2(jnp.where(low, col_a, col_b))
            y_pair = both[:CHUNK] + inter
            state_ref[pair] = prev * cdec_x[:, sl] + both[CHUNK:]
            if finalize:
                y_pairs.append(y_pair + yf_ref[:, sl] + dx_ref[:, sl] * xs[:, sl])
            else:
                y_ref[:, sl] = y_pair

    if finalize:
        y = jnp.concatenate(y_pairs, axis=1) * _silu(z_ref[...].astype(F32))
        gw = SSD_D_INNER // SSD_GROUPS
        for g in range(SSD_GROUPS):
            sl = slice(g * gw, (g + 1) * gw)
            yg = y[:, sl]
            yg = yg * lax.rsqrt(jnp.mean(yg * yg, axis=-1, keepdims=True) + NORM_EPS)
            y_ref[:, sl] = (yg * ng_ref[:, sl]).astype(y_ref.dtype)


def _dir_vec(layer, direction):
    return pl.BlockSpec((None, None, 1, LANES), lambda i: (layer, direction, 0, 0))


def _ssd_forward(proj, dtraw, conv_w, conv_b, dtb, alog, e64, layer, *, n_ctx):
    t = proj.shape[0]
    n_chunks = t // CHUNK
    ncc = n_ctx // CHUNK
    hb = CHUNK // BF16_SUBLANES
    n_hblk = t // BF16_SUBLANES

    def cur(col, w):
        return pl.BlockSpec((CHUNK, w), lambda i: (i, col // w))

    def prev(col, w):
        return pl.BlockSpec((BF16_SUBLANES, w), lambda i: (jnp.maximum(i * hb - 1, 0), col // w))

    def nxt(col, w):
        return pl.BlockSpec((BF16_SUBLANES, w), lambda i: (jnp.minimum((i + 1) * hb, n_hblk - 1), col // w))

    def cw(rows, col, w):
        return pl.BlockSpec((None, rows, w), lambda i: (layer, 0, col // w))

    in_specs = [
        prev(PL_XS, SSD_D_INNER), cur(PL_XS, SSD_D_INNER), nxt(PL_XS, SSD_D_INNER),
        prev(PL_B, SSD_BC), cur(PL_B, SSD_BC), nxt(PL_B, SSD_BC),
        prev(PL_C, SSD_BC), cur(PL_C, SSD_BC), nxt(PL_C, SSD_BC),
        cw(8, 0, SSD_D_INNER), cw(8, SSD_D_INNER, SSD_BC), cw(8, SSD_D_INNER + SSD_BC, SSD_BC),
        cw(1, 0, SSD_D_INNER), cw(1, SSD_D_INNER, SSD_BC), cw(1, SSD_D_INNER + SSD_BC, SSD_BC),
        pl.BlockSpec((CHUNK, LANES), lambda i: (i, 0)),
        _dir_vec(layer, 0), _dir_vec(layer, 0),
        pl.BlockSpec((LANES, SSD_D_INNER), lambda i: (0, 0)),
    ]
    row = lambda w: pl.BlockSpec((CHUNK, w), lambda i: (i, 0))
    out_shape = [
        jax.ShapeDtypeStruct((t, SSD_D_INNER), F32),
        jax.ShapeDtypeStruct((t, SSD_D_INNER), BF16),
        jax.ShapeDtypeStruct((t, SSD_BC), BF16),
        jax.ShapeDtypeStruct((t, SSD_BC), BF16),
    ]
    return pl.pallas_call(
        functools.partial(_ssd_body, backward=False, finalize=False, n_ctx_chunks=ncc, n_chunks=n_chunks),
        grid=(n_chunks,),
        in_specs=in_specs,
        out_specs=[row(SSD_D_INNER), row(SSD_D_INNER), row(SSD_BC), row(SSD_BC)],
        out_shape=out_shape,
        scratch_shapes=[
            pltpu.VMEM((SSD_HEADS // 2, SSD_STATE, LANES), F32),
            pltpu.VMEM((CHUNK + 16, SSD_D_INNER), F32),
            pltpu.VMEM((CHUNK + 16, SSD_BC), F32),
        ],
        compiler_params=_cparams(("arbitrary",)),
        name="ssd_fwd",
    )(proj, proj, proj, proj, proj, proj, proj, proj, proj,
      conv_w, conv_w, conv_w, conv_b, conv_b, conv_b, dtraw, dtb, alog, e64)


def _ssd_backward(proj, xs, bm, cm, dtraw, dtb, alog, e64, yf, d_x, norm_g, layer, *, n_ctx):
    t = proj.shape[0]
    n_chunks = t // CHUNK
    ncc = n_ctx // CHUNK
    cmap = functools.partial(_chunk_of_step, backward=True, n_ctx_chunks=ncc, n_chunks=n_chunks)
    row = lambda w, col=0: pl.BlockSpec((CHUNK, w), lambda i: (cmap(i), col // w))
    in_specs = [
        row(SSD_D_INNER), row(SSD_BC), row(SSD_BC),
        pl.BlockSpec((CHUNK, LANES), lambda i: (cmap(i), 1)),
        _dir_vec(layer, 1), _dir_vec(layer, 1),
        pl.BlockSpec((LANES, SSD_D_INNER), lambda i: (0, 0)),
        row(SSD_D_INNER), row(SSD_D_INNER, PL_Z), _layer_vec(layer, SSD_D_INNER), _layer_vec(layer, SSD_D_INNER),
    ]
    return pl.pallas_call(
        functools.partial(_ssd_body, backward=True, finalize=True, n_ctx_chunks=ncc, n_chunks=n_chunks),
        grid=(n_chunks,),
        in_specs=in_specs,
        out_specs=row(SSD_D_INNER),
        out_shape=jax.ShapeDtypeStruct((t, SSD_D_INNER), BF16),
        scratch_shapes=[pltpu.VMEM((SSD_HEADS // 2, SSD_STATE, LANES), F32)],
        compiler_params=_cparams(("arbitrary",)),
        name="ssd_bwd",
    )(xs, bm, cm, dtraw, dtb, alog, e64, yf, proj, d_x, norm_g)


def _ret_body(*refs, layer, backward, finalize):
    if finalize:
        dec_ref, q_ref, k_ref, v_ref, yf_ref, g_ref, y_ref, state_ref = refs
    else:
        dec_ref, q_ref, k_ref, v_ref, y_ref, state_ref = refs
    i = pl.program_id(0)

    @pl.when(i == 0)
    def _():
        state_ref[...] = jnp.zeros_like(state_ref)

    mask = _tri(backward)
    l_i = lax.broadcasted_iota(I32, (CHUNK, CHUNK), 0)
    s_i = lax.broadcasted_iota(I32, (CHUNK, CHUNK), 1)
    dist = ((s_i - l_i) if backward else (l_i - s_i)).astype(F32)
    pos = lax.broadcasted_iota(I32, (CHUNK, 1), 0).astype(F32)
    srow = lax.broadcasted_iota(I32, (1, CHUNK), 1).astype(F32)
    steps_in = (CHUNK - pos) if backward else (pos + 1.0)
    steps_out = srow if backward else (CHUNK - 1.0 - srow)
    d = 1 if backward else 0
    for h in range(RET_HEADS):
        a = dec_ref[layer, d, h]
        qh = q_ref[:, h * RET_QK_DIM:(h + 1) * RET_QK_DIM]
        kh = k_ref[:, h * RET_QK_DIM:(h + 1) * RET_QK_DIM]
        vh = v_ref[:, h * RET_V_DIM:(h + 1) * RET_V_DIM]
        scores = lax.dot_general(qh, kh, (((1,), (1,)), ((), ())), preferred_element_type=F32)
        m = scores * jnp.exp(jnp.where(mask, a * dist, -jnp.inf))
        intra = jnp.dot(m.astype(BF16), vh, preferred_element_type=F32)
        prev = state_ref[h]
        inter = jnp.dot(qh, prev.astype(BF16), preferred_element_type=F32) * jnp.exp(a * steps_in)
        kw = (kh.astype(F32).T * jnp.exp(a * steps_out)).astype(BF16)
        carry = jnp.exp(a * jnp.full((1, RET_V_DIM), float(CHUNK), F32))
        state_ref[h] = prev * carry + jnp.dot(kw, vh, preferred_element_type=F32)
        y = intra + inter
        sl = slice(h * RET_V_DIM, (h + 1) * RET_V_DIM)
        if finalize:
            y = y + yf_ref[:, sl]
            y = y - jnp.mean(y, axis=-1, keepdims=True)
            y = y * lax.rsqrt(jnp.mean(y * y, axis=-1, keepdims=True) + NORM_EPS)
            y_ref[:, sl] = (y * _silu(g_ref[:, sl].astype(F32))).astype(y_ref.dtype)
        else:
            y_ref[:, sl] = y


def _retention(proj_r, proj_p, log_decay, yf, layer, *, backward, n_ctx):
    t = proj_r.shape[0]
    n_chunks = t // CHUNK
    ncc = n_ctx // CHUNK
    cmap = functools.partial(_chunk_of_step, backward=backward, n_ctx_chunks=ncc, n_chunks=n_chunks)
    row = lambda w, col=0: pl.BlockSpec((CHUNK, w), lambda i: (cmap(i), col // w))
    in_specs = [pl.BlockSpec(memory_space=pltpu.SMEM), row(RET_Q, ROPE_RQ), row(RET_Q, ROPE_RK), row(RET_V, PL_RV)]
    args = [log_decay, proj_r, proj_r, proj_p]
    if backward:
        in_specs += [row(RET_V), row(RET_V, PL_RG)]
        args += [yf, proj_p]
    return pl.pallas_call(
        functools.partial(_ret_body, layer=layer, backward=backward, finalize=backward),
        grid=(n_chunks,),
        in_specs=in_specs,
        out_specs=row(RET_V),
        out_shape=jax.ShapeDtypeStruct((t, RET_V), BF16 if backward else F32),
        scratch_shapes=[pltpu.VMEM((RET_HEADS, RET_QK_DIM, RET_V_DIM), F32)],
        compiler_params=_cparams(("arbitrary",)),
        name="ret_bwd" if backward else "ret_fwd",
    )(*args)


def _attn_body(sink_ref, q_ref, kc_ref, vc_ref, kp_ref, kq_ref, kn_ref, vp_ref, vq_ref, vn_ref, o_ref,
               *, layer, n_ctx_chunks, n_chunks, n_ctx):
    i = pl.program_id(0)
    is_lat = i >= n_ctx_chunks
    rows = ATTN_GRP * CHUNK
    qi = lax.broadcasted_iota(I32, (rows, CHUNK), 0) % CHUNK
    kj = lax.broadcasted_iota(I32, (rows, CHUNK), 1)
    ok_prev = kj >= qi + jnp.where(i > n_ctx_chunks, 0, CHUNK)
    ok_cur = kj >= jnp.where(is_lat, 0, CHUNK)
    ok_next = kj <= qi - jnp.where(is_lat & (i < n_chunks - 1), 0, CHUNK)
    scale = ATTN_HEAD_DIM ** -0.5
    to_exp2 = scale * math.log2(math.e)
    for hk in range(ATTN_KV_HEADS):
        ksl = slice(hk * ATTN_HEAD_DIM, (hk + 1) * ATTN_HEAD_DIM)
        q = jnp.concatenate(
            [q_ref[:, (hk * ATTN_GRP + g) * ATTN_HEAD_DIM:(hk * ATTN_GRP + g + 1) * ATTN_HEAD_DIM]
             for g in range(ATTN_GRP)], axis=0)
        sink = jnp.concatenate(
            [jnp.full((CHUNK, 1), sink_ref[layer, hk * ATTN_GRP + g] * (1.0 / scale), F32)
             for g in range(ATTN_GRP)], axis=0)
        qk = lambda k: lax.dot_general(q, k, (((1,), (1,)), ((), ())), preferred_element_type=F32)
        s_c = qk(kc_ref[:, ksl])
        segs = [s_c[:, c * LANES:(c + 1) * LANES] for c in range(n_ctx // LANES)]
        segs.append(jnp.where(ok_prev, qk(kp_ref[:, ksl]), NEG_INF))
        segs.append(jnp.where(ok_cur, qk(kq_ref[:, ksl]), NEG_INF))
        segs.append(jnp.where(ok_next, qk(kn_ref[:, ksl]), NEG_INF))
        m_el = functools.reduce(jnp.maximum, segs)
        m = jnp.maximum(jnp.max(m_el, axis=-1, keepdims=True), sink)
        ps = [jnp.exp2((s - m) * to_exp2) for s in segs]
        den = jnp.exp2((sink - m) * to_exp2) + jnp.sum(functools.reduce(jnp.add, ps), axis=-1, keepdims=True)
        nc = n_ctx // LANES
        p_c = jnp.concatenate(ps[:nc], axis=1).astype(BF16)
        pv = lambda p, v: jnp.dot(p, v, preferred_element_type=F32)
        o = (pv(p_c, vc_ref[:, ksl]) + pv(ps[nc].astype(BF16), vp_ref[:, ksl])
             + pv(ps[nc + 1].astype(BF16), vq_ref[:, ksl]) + pv(ps[nc + 2].astype(BF16), vn_ref[:, ksl])) / den
        for g in range(ATTN_GRP):
            h = hk * ATTN_GRP + g
            o_ref[:, h * ATTN_HEAD_DIM:(h + 1) * ATTN_HEAD_DIM] = o[g * CHUNK:(g + 1) * CHUNK].astype(o_ref.dtype)


def _attention(proj_r, proj_p, sink, layer, *, n_ctx):
    t = proj_r.shape[0]
    n_chunks = t // CHUNK
    ncc = n_ctx // CHUNK
    kcol = ROPE_AK // ATTN_KV
    vcol = PL_AV // ATTN_KV
    lat = lambda i, off: jnp.clip(i + off, ncc, n_chunks - 1)
    in_specs = [
        pl.BlockSpec(memory_space=pltpu.SMEM),
        pl.BlockSpec((CHUNK, ATTN_Q), lambda i: (i, ROPE_AQ // ATTN_Q)),
        pl.BlockSpec((n_ctx, ATTN_KV), lambda i: (0, kcol)),
        pl.BlockSpec((n_ctx, ATTN_KV), lambda i: (0, vcol)),
        pl.BlockSpec((CHUNK, ATTN_KV), lambda i: (lat(i, -1), kcol)),
        pl.BlockSpec((CHUNK, ATTN_KV), lambda i: (i, kcol)),
        pl.BlockSpec((CHUNK, ATTN_KV), lambda i: (lat(i, 1), kcol)),
        pl.BlockSpec((CHUNK, ATTN_KV), lambda i: (lat(i, -1), vcol)),
        pl.BlockSpec((CHUNK, ATTN_KV), lambda i: (i, vcol)),
        pl.BlockSpec((CHUNK, ATTN_KV), lambda i: (lat(i, 1), vcol)),
    ]
    return pl.pallas_call(
        functools.partial(_attn_body, layer=layer, n_ctx_chunks=ncc, n_chunks=n_chunks, n_ctx=n_ctx),
        grid=(n_chunks,),
        in_specs=in_specs,
        out_specs=pl.BlockSpec((CHUNK, ATTN_Q), lambda i: (i, 0)),
        out_shape=jax.ShapeDtypeStruct((t, ATTN_Q), BF16),
        compiler_params=_cparams(("parallel",)),
        name="attention",
    )(sink, proj_r, proj_r, proj_p, proj_r, proj_r, proj_r, proj_p, proj_p, proj_p)


def _merge_body(ya_ref, yb_ref, yc_ref, gd_ref, wa_ref, wb_ref, wc_ref, ga_ref, gb_ref, gc_ref, *rest):
    n_jobs = (len(rest) - 1) // 2
    o_ref = rest[n_jobs]
    _run_cast_jobs(rest[:n_jobs], rest[n_jobs + 1:])
    gd = gd_ref[...]

    def branch(y_ref, w_ref, g_ref):
        gate = jax.nn.sigmoid(jnp.dot(gd, g_ref[...], preferred_element_type=F32))
        return gate * jnp.dot(y_ref[...], w_ref[...], preferred_element_type=F32)

    o_ref[...] = (branch(ya_ref, wa_ref, ga_ref) + branch(yb_ref, wb_ref, gb_ref)
                  + branch(yc_ref, wc_ref, gc_ref)).astype(o_ref.dtype)


def _weight_spec(w, layer, rows, tn, col_block):
    if w.ndim == 3:
        return pl.BlockSpec((None, rows, tn), lambda i, j: (layer, 0, col_block(j)))
    return pl.BlockSpec((rows, tn), lambda i, j: (0, col_block(j)))


def _merge(ya, yb, yc, proj_p, wa, wb, wc, wg, layer, moe_w2, *, tm, tn=512):
    t = ya.shape[0]
    d = wa.shape[-1]
    ni, nj = t // tm, d // tn
    act = lambda w: pl.BlockSpec((tm, w), lambda i, j: (i, 0))
    wsp = lambda w: _weight_spec(w, layer, w.shape[-2], tn, lambda j: j)
    gsp = lambda b: _weight_spec(wg, layer, GATE_RANK, tn, lambda j: b * nj + j)
    jobs = [_expert_cast_job(moe_w2, layer, ni * nj, nj, False)]
    outs = pl.pallas_call(
        _merge_body,
        grid=(ni, nj),
        in_specs=[act(SSD_D_INNER), act(RET_V), act(ATTN_Q),
                  pl.BlockSpec((tm, GATE_RANK), lambda i, j: (i, PL_GD // GATE_RANK)),
                  wsp(wa), wsp(wb), wsp(wc), gsp(0), gsp(1), gsp(2)] + [jb[1] for jb in jobs],
        out_specs=[pl.BlockSpec((tm, tn), lambda i, j: (i, j))] + [jb[3] for jb in jobs],
        out_shape=[jax.ShapeDtypeStruct((t, d), BF16)] + [jb[2] for jb in jobs],
        compiler_params=_cparams(("arbitrary", "arbitrary")),
        name="merge",
    )(ya, yb, yc, proj_p, wa, wb, wc, wg, wg, wg, *[jb[0] for jb in jobs])
    return outs[0], outs[1]


def _outproj_body(m_ref, w_ref, x_ref, mod_ref, o_ref, *, n_ctx, tm):
    acc = jnp.dot(m_ref[...], w_ref[...], preferred_element_type=F32)
    is_ctx = _is_ctx_rows(pl.program_id(0) * tm, tm, n_ctx)
    gate = jnp.where(is_ctx, mod_ref[0:1, :], mod_ref[1:2, :])
    o_ref[...] = x_ref[...] + gate * acc


def _outproj(merged, w, x, mods, layer, *, n_ctx, tm, tn=512):
    t, k = merged.shape
    d = w.shape[2]
    ni, nj = t // tm, d // tn
    k_gate = 2
    (out,) = pl.pallas_call(
        functools.partial(_outproj_body, n_ctx=n_ctx, tm=tm),
        grid=(ni, nj),
        in_specs=[
            pl.BlockSpec((tm, k), lambda i, j: (i, 0)),
            pl.BlockSpec((None, k, tn), lambda i, j: (layer, 0, j)),
            pl.BlockSpec((tm, tn), lambda i, j: (i, j)),
            pl.BlockSpec((None, 16, tn), lambda i, j: (layer, 0, k_gate * nj + j)),
        ],
        out_specs=[pl.BlockSpec((tm, tn), lambda i, j: (i, j))],
        out_shape=[jax.ShapeDtypeStruct((t, d), F32)],
        compiler_params=_cparams(("parallel", "arbitrary")),
        name="outproj",
    )(merged, w, x, mods)
    return out


def _router_logits(hb, wr_ref, rb_ref):
    return jnp.dot(hb, wr_ref[...], preferred_element_type=F32) + rb_ref[...]


def _router_body(x_ref, g_ref, mod_ref, wr_ref, rb_ref, gi_ref, *, n_ctx, tm):
    h = _norm_mod(x_ref[...], g_ref, mod_ref, 3, 4, pl.program_id(0) * tm, n_ctx)
    logits = _router_logits(h.astype(BF16), wr_ref, rb_ref)
    lane = lax.broadcasted_iota(I32, (tm, LANES), 1)
    is_g = (lane >= N_EXPERTS) & (lane < N_EXPERTS + N_GROUPS)
    gl = jnp.where(is_g, logits, -jnp.inf)
    gmax = jnp.max(gl, axis=-1, keepdims=True)
    first = jnp.min(jnp.where(gl == gmax, (lane - N_EXPERTS).astype(F32), float(N_GROUPS)), axis=-1, keepdims=True)
    gi_ref[...] = first.astype(I32)


def _router(x, g, mods, wr, rb, layer, *, n_ctx, tm=ROW_TM):
    t, d = x.shape
    return pl.pallas_call(
        functools.partial(_router_body, n_ctx=n_ctx, tm=tm),
        grid=(t // tm,),
        in_specs=[
            pl.BlockSpec((tm, d), lambda i: (i, 0)),
            _layer_vec(layer, d), _layer_mod(layer, d),
            pl.BlockSpec((None, d, LANES), lambda i: (layer, 0, 0)),
            _layer_vec(layer, LANES),
        ],
        out_specs=pl.BlockSpec((tm, 1), lambda i: (i, 0)),
        out_shape=jax.ShapeDtypeStruct((t, 1), I32),
        compiler_params=_cparams(("parallel",)),
        name="moe_router",
    )(x, g, mods, wr, rb)


def _positions_body(g_ref, pos_ref, tg_ref, *, n_rows, tm):
    gidx = g_ref[...]
    li = lax.broadcasted_iota(I32, (LANES, LANES), 0)
    lj = lax.broadcasted_iota(I32, (LANES, LANES), 1)
    upper = jnp.where(li <= lj, 1.0, 0.0).astype(BF16)
    ri = lax.broadcasted_iota(I32, (n_rows, n_rows), 0)
    rj = lax.broadcasted_iota(I32, (n_rows, n_rows), 1)
    strict = jnp.where(rj < ri, 1.0, 0.0).astype(BF16)
    tile_start = (lax.broadcasted_iota(I32, (8, LANES), 1) * tm).astype(F32)
    pos = jnp.zeros((n_rows, LANES), F32)
    tg = jnp.zeros((8, LANES), F32)
    off = jnp.zeros((1, 1), F32)
    for g in range(N_GROUPS):
        mk = jnp.where(gidx == g, 1.0, 0.0)
        within = jnp.dot(mk.astype(BF16), upper, preferred_element_type=F32)
        rowtot = jnp.broadcast_to(within[:, LANES - 1:LANES], (n_rows, LANES))
        before = jnp.dot(strict, rowtot.astype(BF16), preferred_element_type=F32)
        pos = pos + mk * (off + before + within - 1.0)
        count = jnp.sum(jnp.sum(mk, axis=-1, keepdims=True), axis=0, keepdims=True)
        padded = jnp.floor((count + (tm - 1.0)) * (1.0 / tm)) * tm
        off = off + padded
        tg = tg + jnp.where(tile_start >= off, 1.0, 0.0)
    pos_ref[...] = pos.astype(I32)
    lane = lax.broadcasted_iota(I32, (8, LANES), 1)
    n_used = off * (1.0 / tm)
    tgi = jnp.minimum(tg, N_GROUPS - 1.0)
    tg_ref[...] = jnp.where(lane == LANES - 1, n_used, tgi).astype(I32)


def _positions(gidx2d, *, tm):
    n_rows = gidx2d.shape[0]
    return pl.pallas_call(
        functools.partial(_positions_body, n_rows=n_rows, tm=tm),
        out_shape=[jax.ShapeDtypeStruct((n_rows, LANES), I32), jax.ShapeDtypeStruct((8, LANES), I32)],
        name="moe_positions",
    )(gidx2d)


ROW_DMA_UNROLL = 8


def _start_then_wait_rows(row_copy, n_rows):
    def start(r, c):
        row_copy(r).start()
        return c

    def wait(r, c):
        row_copy(r).wait()
        return c

    lax.fori_loop(0, n_rows, start, 0, unroll=ROW_DMA_UNROLL)
    lax.fori_loop(0, n_rows, wait, 0, unroll=ROW_DMA_UNROLL)


def _scatter_body(pos_ref, x_ref, g_ref, mod_ref, init_ref, o_ref, h_ref, sem, *, tm, n_ctx):
    del init_ref
    base = pl.program_id(0) * tm
    h_ref[...] = _norm_mod(x_ref[...], g_ref, mod_ref, 3, 4, base, n_ctx)

    def row_copy(r):
        return pltpu.make_async_copy(h_ref.at[pl.ds(r, 1), :], o_ref.at[pl.ds(pos_ref[base + r], 1), :], sem)

    _start_then_wait_rows(row_copy, tm)


def _scatter_rows(pos, x, g, mods, init, layer, *, n_ctx, tm=ROW_TM):
    t, d = x.shape
    return pl.pallas_call(
        functools.partial(_scatter_body, tm=tm, n_ctx=n_ctx),
        grid_spec=pltpu.PrefetchScalarGridSpec(
            num_scalar_prefetch=1,
            grid=(t // tm,),
            in_specs=[pl.BlockSpec((tm, d), lambda i, p: (i, 0)), _layer_vec(layer, d), _layer_mod(layer, d),
                      pl.BlockSpec(memory_space=pl.ANY)],
            out_specs=pl.BlockSpec(memory_space=pl.ANY),
            scratch_shapes=[pltpu.VMEM((tm, d), F32), pltpu.SemaphoreType.DMA(())],
        ),
        out_shape=jax.ShapeDtypeStruct(init.shape, init.dtype),
        input_output_aliases={4: 0},
        compiler_params=_cparams(("arbitrary",)),
        name="moe_scatter",
    )(pos, x, g, mods, init)


def _experts_body(tg_ref, x_ref, wr_ref, rb_ref, w1_ref, w3_ref, w2_ref, o_ref, xb_ref, gate_ref, *, tm):
    ti = pl.program_id(0)
    q = pl.program_id(1)
    grp = tg_ref[ti]
    used = ti < tg_ref[LANES - 1]
    lane = lax.broadcasted_iota(I32, (tm, LANES), 1)
    lane_f = lane.astype(F32)

    @pl.when(jnp.logical_not(used) & (q == 0))
    def _():
        o_ref[...] = jnp.zeros_like(o_ref)

    @pl.when(used & (q == 0))
    def _():
        xb = x_ref[...].astype(BF16)
        xb_ref[...] = xb
        logits = _router_logits(xb, wr_ref, rb_ref)
        is_g = (lane >= N_EXPERTS) & (lane < N_EXPERTS + N_GROUPS)
        gl = jnp.where(is_g, logits, -jnp.inf)
        gmax = jnp.max(gl, axis=-1, keepdims=True)
        gsum = jnp.sum(jnp.exp(gl - gmax), axis=-1, keepdims=True)
        gsel = jnp.sum(jnp.where(lane == N_EXPERTS + grp, logits, 0.0), axis=-1, keepdims=True)
        p_grp = jnp.exp(gsel - gmax) / gsum
        in_grp = (lane >= grp * EXPERTS_PER_GROUP) & (lane < (grp + 1) * EXPERTS_PER_GROUP)
        el = jnp.where(in_grp, logits, -jnp.inf)
        v1 = jnp.max(el, axis=-1, keepdims=True)
        i1 = jnp.min(jnp.where(el == v1, lane_f, float(LANES)), axis=-1, keepdims=True)
        el2 = jnp.where(lane_f == i1, -jnp.inf, el)
        v2 = jnp.max(el2, axis=-1, keepdims=True)
        i2 = jnp.min(jnp.where(el2 == v2, lane_f, float(LANES)), axis=-1, keepdims=True)
        e2 = jnp.exp(v2 - v1)
        w1 = p_grp / (1.0 + e2)
        w2 = p_grp * e2 / (1.0 + e2)
        gate_ref[...] = jnp.where(lane_f == i1, w1, 0.0) + jnp.where(lane_f == i2, w2, 0.0)

    @pl.when(used)
    def _():
        xb = xb_ref[...]
        a = jnp.dot(xb, w1_ref[...], preferred_element_type=F32)
        u = jnp.dot(xb, w3_ref[...], preferred_element_type=F32)
        gates = gate_ref[...]
        per = MOE_SLAB // EXPERT_FF
        cols = []
        for j in range(per):
            e = grp * EXPERTS_PER_GROUP + q * per + j
            ge = jnp.sum(jnp.where(lane == e, gates, 0.0), axis=-1, keepdims=True)
            cols.append(jnp.broadcast_to(ge, (tm, EXPERT_FF)))
        hid = (_silu(a) * u * jnp.concatenate(cols, axis=1)).astype(BF16)
        part = jnp.dot(hid, w2_ref[...], preferred_element_type=F32)

        @pl.when(q == 0)
        def _():
            o_ref[...] = part

        @pl.when(q > 0)
        def _():
            o_ref[...] += part


def _experts(tile_group, xs, wr, rb, w1, w3, w2, layer, *, tm):
    n_rows, d = xs.shape
    n_tiles = n_rows // tm
    return pl.pallas_call(
        functools.partial(_experts_body, tm=tm),
        grid_spec=pltpu.PrefetchScalarGridSpec(
            num_scalar_prefetch=1,
            grid=(n_tiles, MOE_SPLIT),
            in_specs=[
                pl.BlockSpec((tm, d), lambda i, q, tg: (i, 0)),
                pl.BlockSpec((None, d, LANES), lambda i, q, tg: (layer, 0, 0)),
                pl.BlockSpec((None, 1, LANES), lambda i, q, tg: (layer, 0, 0)),
                pl.BlockSpec((d, MOE_SLAB), lambda i, q, tg: (0, tg[i] * MOE_SPLIT + q)),
                pl.BlockSpec((d, MOE_SLAB), lambda i, q, tg: (0, tg[i] * MOE_SPLIT + q)),
                pl.BlockSpec((MOE_SLAB, d), lambda i, q, tg: (tg[i] * MOE_SPLIT + q, 0)),
            ],
            out_specs=pl.BlockSpec((tm, d), lambda i, q, tg: (i, 0)),
            scratch_shapes=[pltpu.VMEM((tm, d), BF16), pltpu.VMEM((tm, LANES), F32)],
        ),
        out_shape=jax.ShapeDtypeStruct((n_rows, d), F32),
        compiler_params=_cparams(("arbitrary", "arbitrary")),
        name="moe_experts",
    )(tile_group, xs, wr, rb, w1, w3, w2)


def _combine_body(pos_ref, x_ref, mod_ref, ys_ref, g_ref, modn_ref, *rest, tm, n_ctx, first_row, final):
    if final:
        o_ref, buf_ref, sem = rest
    else:
        o_ref, h_ref, buf_ref, sem = rest
    base = first_row + pl.program_id(0) * tm

    def row_copy(r):
        return pltpu.make_async_copy(ys_ref.at[pl.ds(pos_ref[base + r], 1), :], buf_ref.at[pl.ds(r, 1), :], sem)

    _start_then_wait_rows(row_copy, tm)
    x = x_ref[...] + _mod_row(mod_ref, 5, base, n_ctx) * buf_ref[...]
    if final:
        o_ref[...] = x * lax.rsqrt(jnp.mean(x * x, axis=-1, keepdims=True) + NORM_EPS) * g_ref[...]
    else:
        o_ref[...] = x
        h_ref[...] = _norm_mod(x, g_ref, modn_ref, 0, 1, base, n_ctx).astype(h_ref.dtype)


def _combine(pos, x, mods, ys, g_next, layer, *, n_ctx, final, tm=ROW_TM):
    t, d = x.shape
    first_tile = n_ctx // tm if final else 0
    n_out = t - first_tile * tm
    next_layer = layer if final else layer + 1
    g_spec = pl.BlockSpec((1, d), lambda i, p: (0, 0)) if final else _layer_vec(next_layer, d)
    out_specs = [pl.BlockSpec((tm, d), lambda i, p: (i, 0))]
    out_shape = [jax.ShapeDtypeStruct((n_out, d), F32)]
    if not final:
        out_specs.append(pl.BlockSpec((tm, d), lambda i, p: (i, 0)))
        out_shape.append(jax.ShapeDtypeStruct((t, d), BF16))
    return pl.pallas_call(
        functools.partial(_combine_body, tm=tm, n_ctx=n_ctx, first_row=first_tile * tm, final=final),
        grid_spec=pltpu.PrefetchScalarGridSpec(
            num_scalar_prefetch=1,
            grid=(n_out // tm,),
            in_specs=[
                pl.BlockSpec((tm, d), lambda i, p: (i + first_tile, 0)),
                _layer_mod(layer, d),
                pl.BlockSpec(memory_space=pl.ANY),
                g_spec,
                _layer_mod(next_layer, d),
            ],
            out_specs=out_specs,
            scratch_shapes=[pltpu.VMEM((tm, d), F32), pltpu.SemaphoreType.DMA(())],
        ),
        out_shape=out_shape,
        compiler_params=_cparams(("arbitrary",)),
        name="moe_combine_final" if final else "moe_combine",
    )(pos, x, mods, ys, g_next, mods)


_W_IN_OFF = tuple(sum(IN_SPLITS[:k]) for k in range(len(IN_SPLITS)))
_SEG_Z, _SEG_XBC, _SEG_DT, _SEG_RQ, _SEG_RK, _SEG_RV, _SEG_RG, _SEG_AQ, _SEG_AK, _SEG_AV, _SEG_GD = range(len(IN_SPLITS))
_ROPE_SRC = ((_SEG_AQ, 0, ATTN_Q), (_SEG_RQ, 0, RET_Q), (_SEG_RK, 0, RET_Q), (_SEG_AK, 0, ATTN_KV))
_PLAIN_SRC = ((_SEG_Z, 0, SSD_D_INNER), (_SEG_XBC, 0, SSD_D_INNER), (_SEG_RV, 0, RET_V), (_SEG_RG, 0, RET_V),
              (_SEG_AV, 0, ATTN_KV), (_SEG_XBC, SSD_D_INNER, SSD_BC), (_SEG_XBC, SSD_D_INNER + SSD_BC, SSD_BC),
              (_SEG_GD, 0, GATE_RANK))
W_PREP_SRC_ROWS = 64
W_PREP_OUT_ROWS = 256
W_PREP_SUB = W_PREP_OUT_ROWS // W_PREP_SRC_ROWS


def _prep_source_blocks():
    table = []
    for seg, inner, width in _ROPE_SRC + _PLAIN_SRC:
        src = _W_IN_OFF[seg] + inner
        assert src % W_PREP_SRC_ROWS == 0 and width % W_PREP_OUT_ROWS == 0
        table += [(src + o) // W_PREP_SRC_ROWS for o in range(0, width, W_PREP_OUT_ROWS)]
    return table


def _prep_w_in_body(tbl_ref, *refs):
    del tbl_ref
    o_ref = refs[W_PREP_SUB]
    for u in range(W_PREP_SUB):
        o_ref[u * W_PREP_SRC_ROWS:(u + 1) * W_PREP_SRC_ROWS, :] = refs[u][...].astype(o_ref.dtype)


def _prep_w_in(w_t):
    depth, _, k = w_t.shape
    table = jnp.asarray(_prep_source_blocks(), I32)
    n_blocks = (ROPE_W + PLAIN_W) // W_PREP_OUT_ROWS
    src = lambda u: pl.BlockSpec((None, W_PREP_SRC_ROWS, k), lambda l, b, tbl: (l, tbl[b] + u, 0))
    return pl.pallas_call(
        _prep_w_in_body,
        grid_spec=pltpu.PrefetchScalarGridSpec(
            num_scalar_prefetch=1,
            grid=(depth, n_blocks),
            in_specs=[src(u) for u in range(W_PREP_SUB)],
            out_specs=pl.BlockSpec((None, W_PREP_OUT_ROWS, k), lambda l, b, tbl: (l, b, 0)),
        ),
        out_shape=jax.ShapeDtypeStruct((depth, ROPE_W + PLAIN_W, k), BF16),
        compiler_params=_cparams(("arbitrary", "arbitrary")),
        name="prep_w_in",
    )(table, *([w_t] * W_PREP_SUB))


def _rope_tables(n_ctx, seq):
    n_rows = seq // GRID_W
    row = jnp.repeat(jnp.arange(n_rows), GRID_W).astype(F32)
    col = jnp.tile(jnp.arange(GRID_W), n_rows).astype(F32)
    n_freq = ROPE_DIM // 4
    inv = ROPE_BASE ** (-jnp.arange(n_freq, dtype=F32) / n_freq)
    ang = jnp.concatenate([row[:, None] * inv, row[:, None] * inv, col[:, None] * inv, col[:, None] * inv], axis=-1)
    sign = jnp.where((jnp.arange(ROPE_DIM) % (ROPE_DIM // 2)) < ROPE_DIM // 4, -1.0, 1.0).astype(F32)
    cos = jnp.concatenate([jnp.ones((n_ctx, ROPE_DIM), F32), jnp.cos(ang)], axis=0)
    sin = jnp.concatenate([jnp.zeros((n_ctx, ROPE_DIM), F32), jnp.sin(ang) * sign], axis=0)
    return cos, sin


def _row_tile(t):
    for tm in (1056, 1024, 768, 512, 384, 256):
        if t % tm == 0:
            return tm
    raise ValueError(f"token count {t} is not a multiple of 256")


def kernel(x, c, ctx, c_ctx, norm1_g, norm2_g, mod_down, mod_up, mod_b, w_in, conv_w, conv_b, ssd_a_log, ssd_dt_bias, ssd_d, ssd_norm_g, ret_log_decay, attn_sink, w_branch_a, w_branch_b, w_branch_c, w_gate_up, w_out, router_group_w, router_group_b, router_expert_w, router_expert_b, moe_w1, moe_w3, moe_w2, final_norm_g):
    assert x.shape[0] == 1 and ctx.shape[0] == 1
    seq, d = x.shape[1], x.shape[2]
    n_ctx = ctx.shape[1]
    depth = w_in.shape[0]
    assert d == D_MODEL and seq % ROW_TM == 0 and n_ctx % ROW_TM == 0 and seq % GRID_W == 0
    t = n_ctx + seq
    tm = _row_tile(t)
    tm_merge = 768 if t % 768 == 0 else ROW_TM

    w_in_t = jnp.swapaxes(w_in, 1, 2)
    w_proj = _prep_w_in(w_in_t)
    dt_t = w_in_t[:, _W_IN_OFF[_SEG_DT]:_W_IN_OFF[_SEG_DT] + 2 * SSD_HEADS]
    dt_pad = jnp.zeros((depth, LANES - SSD_HEADS, d), F32)
    w_dt = jnp.concatenate([dt_t[:, :SSD_HEADS], dt_pad, dt_t[:, SSD_HEADS:], dt_pad], axis=1).astype(BF16)
    wb =w_branch_b.astype(BF16)
    wc = w_branch_c.astype(BF16)
    wo = w_out.astype(BF16)
    wr = jnp.concatenate([router_expert_w, router_group_w,
                          jnp.zeros((depth, d, LANES - N_EXPERTS - N_GROUPS), F32)], axis=-1).astype(BF16)
    rb = jnp.concatenate([router_expert_b, router_group_b,
                          jnp.zeros((depth, LANES - N_EXPERTS - N_GROUPS), F32)], axis=-1).reshape(depth, 1, LANES)
    conv_w8 = jnp.concatenate([conv_w, jnp.zeros((depth, 8 - SSD_CONV, SSD_XBC), F32)], axis=1)
    conv_b3 = conv_b.reshape(depth, 1, SSD_XBC)
    hpad = jnp.zeros((depth, 2, LANES - SSD_HEADS), F32)
    dtb = jnp.concatenate([ssd_dt_bias, hpad], axis=-1).reshape(depth, 2, 1, LANES)
    alog = jnp.concatenate([ssd_a_log, hpad], axis=-1).reshape(depth, 2, 1, LANES)
    d_x = jnp.repeat(ssd_d, SSD_HEAD_DIM, axis=-1).reshape(depth, 1, SSD_D_INNER)
    ssd_g = ssd_norm_g.reshape(depth, 1, SSD_D_INNER)
    g1 = norm1_g.reshape(depth, 1, d)
    g2 = norm2_g.reshape(depth, 1, d)

    xres = jnp.concatenate([ctx[0], x[0]], axis=0)
    cc = jnp.zeros((16, d), F32).at[0].set(c_ctx).at[1].set(c[0])
    mods = _modulation(cc, mod_down, mod_up, mod_b)
    cos, sin = _rope_tables(n_ctx, seq)
    colscale = jnp.ones((1, ROPE_W), F32).at[:, ROPE_RK:ROPE_RK + RET_Q].set(RET_QK_DIM ** -0.5)
    e64 = (jnp.arange(LANES)[:, None] == (jnp.arange(SSD_D_INNER)[None, :] // SSD_HEAD_DIM)).astype(BF16)
    n_idx_rows = -(-(t // LANES) // 8) * 8
    n_sorted = (t // MOE_TM + N_GROUPS) * MOE_TM
    sorted_buf = jnp.zeros((n_sorted, d), F32)

    h = _normmod(xres, g1, mods, 0, n_ctx=n_ctx)
    out = None
    for i in range(depth):
        proj_r, (wa, wg) = _proj_rope(h, w_proj, i, cos, sin, colscale,
                                      [(w_branch_a, 0, 32), (w_gate_up, 1, 8)], tm=tm)
        nj_plain = PLAIN_W // PROJ_TN
        plain_jobs = [_expert_cast_job(moe_w1, i, (t // tm) * nj_plain, nj_plain, True),
                      _expert_cast_job(moe_w3, i, (t // tm) * nj_plain, nj_plain, True)]
        proj_p, (w1, w3) = _matmul(h, w_proj, i, ROPE_W, PLAIN_W, plain_jobs, tm=tm, tn=PROJ_TN,
                                   out_dtype=BF16, name="proj_plain")
        dtraw, _ = _matmul(h, w_dt, i, 0, DT_W, [], tm=tm, tn=DT_W, out_dtype=F32, name="proj_dt")

        yf, xs_c, b_c, c_c = _ssd_forward(proj_p, dtraw, conv_w8, conv_b3, dtb, alog, e64, i, n_ctx=n_ctx)
        ya = _ssd_backward(proj_p, xs_c, b_c, c_c, dtraw, dtb, alog, e64, yf, d_x, ssd_g, i, n_ctx=n_ctx)
        rf = _retention(proj_r, proj_p, ret_log_decay, None, i, backward=False, n_ctx=n_ctx)
        yb = _retention(proj_r, proj_p, ret_log_decay, rf, i, backward=True, n_ctx=n_ctx)
        yc = _attention(proj_r, proj_p, attn_sink, i, n_ctx=n_ctx)

        merged, w2 = _merge(ya, yb, yc, proj_p, wa, wb, wc, wg, i, moe_w2, tm=tm_merge)
        xres = _outproj(merged, wo, xres, mods, i, n_ctx=n_ctx, tm=tm)

        gidx = _router(xres, g2, mods, wr, rb, i, n_ctx=n_ctx)
        gidx2d = jnp.concatenate([gidx[:, 0], jnp.full((n_idx_rows * LANES - t,), N_GROUPS, I32)]).reshape(n_idx_rows, LANES)
        pos2d, tile_group = _positions(gidx2d, tm=MOE_TM)
        pos = pos2d.reshape(-1)[:t]
        sorted_buf = _scatter_rows(pos, xres, g2, mods, sorted_buf, i, n_ctx=n_ctx)
        ys = _experts(tile_group[0], sorted_buf, wr, rb, w1, w3, w2, i, tm=MOE_TM)
        if i + 1 < depth:
            xres, h = _combine(pos, xres, mods, ys, g1, i, n_ctx=n_ctx, final=False)
        else:
            (out,) = _combine(pos, xres, mods, ys, final_norm_g.reshape(1, d), i, n_ctx=n_ctx, final=True)
    return out[None]
```
